```python
import jax, jax.numpy as jnp
from jax import lax
import numpy as np

D_MODEL = 1024
BATCH = 8
SEQ = 2048
DEPTH = 4

MEM_LEN = 256
MIX_WIDTH = D_MODEL
HGRN_WIDTH = MIX_WIDTH // 2
MLSTM_WIDTH = MIX_WIDTH - HGRN_WIDTH
HGRN_HEADS = 4
HGRN_HEAD_DIM = HGRN_WIDTH // HGRN_HEADS
MLSTM_HEADS = 4
MLSTM_HEAD_DIM = MLSTM_WIDTH // MLSTM_HEADS
CONV_WIDTH = 5
XATTN_HEADS = 4
XATTN_HEAD_DIM = D_MODEL // XATTN_HEADS
D_FF = 4 * D_MODEL
CHUNK = 64
NORM_EPS = 1e-6
MLSTM_FGATE_BIAS = 3.0
IN_SPLITS = (HGRN_WIDTH, HGRN_WIDTH, HGRN_WIDTH, HGRN_WIDTH, HGRN_WIDTH,
             MLSTM_WIDTH, MLSTM_WIDTH, MLSTM_WIDTH, MLSTM_WIDTH,
             MLSTM_HEADS, MLSTM_HEADS, MLSTM_HEADS, MLSTM_HEADS)
D_IN = 5 * HGRN_WIDTH + 4 * MLSTM_WIDTH + 4 * MLSTM_HEADS

kernel_name = "bidir_hgrn2_mlstm_hybrid_encoder"


def rmsnorm(x, g):
    x32 = x.astype(jnp.float32)
    y = x32 * lax.rsqrt(jnp.mean(x32 * x32, axis=-1, keepdims=True) + NORM_EPS)
    return (y * g.astype(jnp.float32)).astype(x.dtype)


def head_rmsnorm(h, g, n_heads):
    b, s, w = h.shape
    h32 = h.astype(jnp.float32).reshape(b, s, n_heads, w // n_heads)
    h32 = h32 * lax.rsqrt(jnp.mean(h32 * h32, axis=-1, keepdims=True) + NORM_EPS)
    return (h32.reshape(b, s, w) * g.astype(jnp.float32)).astype(h.dtype)


def split_cols(t, sizes):
    idx = np.cumsum(np.array(sizes))[:-1].tolist()
    return jnp.split(t, idx, axis=-1)


def to_heads(t, n_heads):
    b, s, w = t.shape
    return t.reshape(b, s, n_heads, w // n_heads).transpose(0, 2, 1, 3)


def from_heads(t):
    b, h, s, d = t.shape
    return t.transpose(0, 2, 1, 3).reshape(b, s, h * d)


def to_chunks(t):
    b, h, s = t.shape[:3]
    t = t.reshape(b, h, s // CHUNK, CHUNK, *t.shape[3:])
    return jnp.moveaxis(t, 2, 0)


def from_chunks(t):
    t = jnp.moveaxis(t, 0, 2)
    b, h, n, c = t.shape[:4]
    return t.reshape(b, h, n * c, *t.shape[4:])


def flip_seq(t):
    return jnp.flip(t, axis=2)


def hgrn2_scan(q, k, v, log_f):
    b, h, _, dk = q.shape
    dv = v.shape[-1]
    tril = jnp.tril(jnp.ones((CHUNK, CHUNK), dtype=bool))

    def step(state, xs):
        qc, kc, vc, lfc = xs
        cum = jnp.cumsum(lfc, axis=-2)
        o_inter = jnp.einsum('bhtd,bhde->bhte', qc * jnp.exp(cum), state)
        diff = cum[:, :, :, None, :] - cum[:, :, None, :, :]
        decay = jnp.exp(jnp.where(tril[:, :, None], diff, -jnp.inf))
        scores = jnp.einsum('bhtd,bhsd,bhtsd->bhts', qc, kc, decay)
        o_intra = jnp.einsum('bhts,bhse->bhte', scores, vc)
        cum_last = cum[:, :, -1:, :]
        k_dec = kc * jnp.exp(cum_last - cum)
        state = (jnp.exp(cum_last[:, :, 0, :])[..., None] * state
                 + jnp.einsum('bhsd,bhse->bhde', k_dec, vc))
        return state, o_inter + o_intra

    s0 = jnp.zeros((b, h, dk, dv), jnp.float32)
    xs = tuple(to_chunks(t.astype(jnp.float32)) for t in (q, k, v, log_f))
    _, o = lax.scan(step, s0, xs)
    return from_chunks(o)


def mlstm_scan(q, k, v, ig, log_fg):
    b, h, _, d = q.shape
    tril = jnp.tril(jnp.ones((CHUNK, CHUNK), dtype=bool))

    def step(carry, xs):
        c_st, n_st, m = carry
        qc, kc, vc, igc, lfc = xs
        cum = jnp.cumsum(lfc, axis=-1)
        log_d = cum[..., :, None] - cum[..., None, :] + igc[..., None, :]
        log_d = jnp.where(tril, log_d, -jnp.inf)
        log_inter = cum + m[..., None]
        m_t = jnp.maximum(jnp.max(log_d, axis=-1), log_inter)
        d_mat = jnp.exp(log_d - m_t[..., None])
        w_inter = jnp.exp(log_inter - m_t)
        scores = jnp.einsum('bhtd,bhsd->bhts', qc, kc) * d_mat
        num = (jnp.einsum('bhts,bhse->bhte', scores, vc)
               + w_inter[..., None] * jnp.einsum('bhtd,bhde->bhte', qc, c_st))
        den = jnp.sum(scores, axis=-1) + w_inter * jnp.einsum('bhtd,bhd->bht', qc, n_st)
        h_out = num / jnp.maximum(jnp.abs(den), jnp.exp(-m_t))[..., None]
        cum_last = cum[..., -1]
        log_w = cum_last[..., None] - cum + igc
        m_new = jnp.maximum(cum_last + m, jnp.max(log_w, axis=-1))
        carry_w = jnp.exp(cum_last + m - m_new)
        k_w = kc * jnp.exp(log_w - m_new[..., None])[..., None]
        c_st = carry_w[..., None, None] * c_st + jnp.einsum('bhsd,bhse->bhde', k_w, vc)
        n_st = carry_w[..., None] * n_st + jnp.sum(k_w, axis=-2)
        return (c_st, n_st, m_new), h_out

    init = (jnp.zeros((b, h, d, d), jnp.float32),
            jnp.zeros((b, h, d), jnp.float32),
            jnp.zeros((b, h), jnp.float32))
    xs = tuple(to_chunks(t.astype(jnp.float32)) for t in (q, k, v, ig, log_fg))
    _, hs = lax.scan(step, init, xs)
    return from_chunks(hs)


def layer_lower_bounds(logits):
    p = jax.nn.softmax(logits.astype(jnp.float32), axis=1)
    c = jnp.cumsum(p, axis=1)
    return c - c[:, :1]


def lower_bounded_log_forget(z, lb):
    z = z.astype(jnp.float32)
    return jnp.logaddexp(jnp.log(lb), jnp.log1p(-lb) + jax.nn.log_sigmoid(z))


def centred_dwconv(x, w, b):
    y = lax.conv_general_dilated(
        x, w[:, None, :], window_strides=(1,),
        padding=[(CONV_WIDTH // 2, CONV_WIDTH // 2)],
        dimension_numbers=('NWC', 'WIO', 'NWC'),
        feature_group_count=x.shape[-1])
    return y + b


def token_mixer(xn, w_in, b_in, lb_fwd, lb_bwd, conv_w, conv_b, hgrn_g, mlstm_g, w_out):
    proj = xn @ w_in + b_in
    (h_q, h_f_fwd, h_f_bwd, h_i, h_g,
     m_q, m_k, m_v, m_o,
     m_ig_fwd, m_ig_bwd, m_fg_fwd, m_fg_bwd) = split_cols(proj, IN_SPLITS)

    q_h = to_heads(jax.nn.silu(h_q), HGRN_HEADS)
    v_h = to_heads(h_i, HGRN_HEADS)
    lf_fwd = to_heads(lower_bounded_log_forget(h_f_fwd, lb_fwd), HGRN_HEADS)
    lf_bwd = to_heads(lower_bounded_log_forget(h_f_bwd, lb_bwd), HGRN_HEADS)
    k_fwd = -jnp.expm1(lf_fwd)
    k_bwd = -jnp.expm1(lf_bwd)
    o_h = (hgrn2_scan(q_h, k_fwd, v_h, lf_fwd)
           + flip_seq(hgrn2_scan(flip_seq(q_h), flip_seq(k_bwd), flip_seq(v_h), flip_seq(lf_bwd))))
    hgrn_out = head_rmsnorm(from_heads(o_h).astype(xn.dtype), hgrn_g, HGRN_HEADS) * jax.nn.silu(h_g)

    qk = jax.nn.silu(centred_dwconv(jnp.concatenate([m_q, m_k], axis=-1), conv_w, conv_b))
    m_q, m_k = jnp.split(qk, 2, axis=-1)
    q_m = to_heads(m_q, MLSTM_HEADS)
    k_m = to_heads(m_k, MLSTM_HEADS) * (MLSTM_HEAD_DIM ** -0.5)
    v_m = to_heads(m_v, MLSTM_HEADS)
    ig_fwd = jnp.swapaxes(m_ig_fwd, 1, 2).astype(jnp.float32)
    ig_bwd = jnp.swapaxes(m_ig_bwd, 1, 2).astype(jnp.float32)
    lfg_fwd = jax.nn.log_sigmoid(jnp.swapaxes(m_fg_fwd, 1, 2).astype(jnp.float32))
    lfg_bwd = jax.nn.log_sigmoid(jnp.swapaxes(m_fg_bwd, 1, 2).astype(jnp.float32))
    h_m = (mlstm_scan(q_m, k_m, v_m, ig_fwd, lfg_fwd)
           + flip_seq(mlstm_scan(flip_seq(q_m), flip_seq(k_m), flip_seq(v_m),
                                 flip_seq(ig_bwd), flip_seq(lfg_bwd))))
    mlstm_out = head_rmsnorm(from_heads(h_m).astype(xn.dtype), mlstm_g, MLSTM_HEADS) * jax.nn.sigmoid(m_o)

    return jnp.concatenate([hgrn_out, mlstm_out], axis=-1) @ w_out


def cross_attention(xn, memn, w_q, w_kv, w_o):
    b, s, _ = xn.shape
    q = (xn @ w_q).reshape(b, s, XATTN_HEADS, XATTN_HEAD_DIM)
    k, v = jnp.split(memn @ w_kv, 2, axis=-1)
    k = k.reshape(b, -1, XATTN_HEADS, XATTN_HEAD_DIM)
    v = v.reshape(b, -1, XATTN_HEADS, XATTN_HEAD_DIM)
    scores = jnp.einsum('bqhd,bkhd->bhqk', q, k).astype(jnp.float32) * (XATTN_HEAD_DIM ** -0.5)
    p = jax.nn.softmax(scores, axis=-1).astype(v.dtype)
    o = jnp.einsum('bhqk,bkhd->bqhd', p, v).reshape(b, s, D_MODEL)
    return o @ w_o


def squared_relu_mlp(xn, w_up, w_down):
    return jnp.square(jax.nn.relu(xn @ w_up)) @ w_down


def setup_inputs(seed: int = 0) -> dict:
    key = jax.random.key(seed)
    ks = jax.random.split(key, 20)

    def nrm(k, shape, scale):
        return jax.random.normal(k, shape, jnp.float32) * scale

    def gain(k, shape):
        return 1.0 + 0.05 * jax.random.normal(k, shape, jnp.float32)

    fgate_offset = jnp.concatenate([jnp.zeros((D_IN - 2 * MLSTM_HEADS,), jnp.float32),
                                    jnp.full((2 * MLSTM_HEADS,), MLSTM_FGATE_BIAS, jnp.float32)])
    return {
        "x": nrm(ks[0], (BATCH, SEQ, D_MODEL), 1.0),
        "mem": nrm(ks[1], (BATCH, MEM_LEN, D_MODEL), 1.0),
        "norm_mix": gain(ks[2], (DEPTH, D_MODEL)),
        "norm_xattn": gain(ks[3], (DEPTH, D_MODEL)),
        "norm_mem": gain(ks[4], (DEPTH, D_MODEL)),
        "norm_mlp": gain(ks[5], (DEPTH, D_MODEL)),
        "norm_final": gain(ks[6], (D_MODEL,)),
        "w_in": nrm(ks[7], (DEPTH, D_MODEL, D_IN), D_MODEL ** -0.5),
        "b_in": nrm(ks[8], (DEPTH, D_IN), 0.02) + fgate_offset,
        "hgrn_lb_logits": nrm(ks[9], (2, DEPTH, HGRN_WIDTH), 0.5),
        "hgrn_norm": gain(ks[10], (DEPTH, HGRN_WIDTH)),
        "mlstm_conv_w": nrm(ks[11], (DEPTH, CONV_WIDTH, 2 * MLSTM_WIDTH), CONV_WIDTH ** -0.5),
        "mlstm_conv_b": nrm(ks[12], (DEPTH, 2 * MLSTM_WIDTH), 0.02),
        "mlstm_norm": gain(ks[13], (DEPTH, MLSTM_WIDTH)),
        "w_out": nrm(ks[14], (DEPTH, MIX_WIDTH, D_MODEL), MIX_WIDTH ** -0.5),
        "w_xq": nrm(ks[15], (DEPTH, D_MODEL, D_MODEL), D_MODEL ** -0.5),
        "w_xkv": nrm(ks[16], (DEPTH, D_MODEL, 2 * D_MODEL), D_MODEL ** -0.5),
        "w_xo": nrm(ks[17], (DEPTH, D_MODEL, D_MODEL), D_MODEL ** -0.5),
        "w_up": nrm(ks[18], (DEPTH, D_MODEL, D_FF), D_MODEL ** -0.5),
        "w_down": nrm(ks[19], (DEPTH, D_FF, D_MODEL), D_FF ** -0.5),
    }


def reference(x, mem, norm_mix, norm_xattn, norm_mem, norm_mlp, norm_final, w_in, b_in,
              hgrn_lb_logits, hgrn_norm, mlstm_conv_w, mlstm_conv_b, mlstm_norm, w_out,
              w_xq, w_xkv, w_xo, w_up, w_down):
    lb = layer_lower_bounds(hgrn_lb_logits)
    for l in range(DEPTH):
        x = x + token_mixer(rmsnorm(x, norm_mix[l]), w_in[l], b_in[l], lb[0, l], lb[1, l],
                            mlstm_conv_w[l], mlstm_conv_b[l], hgrn_norm[l], mlstm_norm[l], w_out[l])
        x = x + cross_attention(rmsnorm(x, norm_xattn[l]), rmsnorm(mem, norm_mem[l]),
                                w_xq[l], w_xkv[l], w_xo[l])
        x = x + squared_relu_mlp(rmsnorm(x, norm_mlp[l]), w_up[l], w_down[l])
    return rmsnorm(x, norm_final)
```

```python
import functools

import jax
import jax.numpy as jnp
from jax import lax
from jax.experimental import pallas as pl
from jax.experimental.pallas import tpu as pltpu

F32 = jnp.float32
BF16 = jnp.bfloat16

D_MODEL = 1024
HEADS = 4
HEAD_DIM = 128
GROUP_WIDTH = HEADS * HEAD_DIM
CONV_WIDTH = 5
XATTN_HEAD_DIM = D_MODEL // HEADS
D_FF = 4 * D_MODEL
CHUNK = 64
SUB = 16
NORM_EPS = 1e-6
MAIN_COLS = 9 * GROUP_WIDTH
CONV_HALO = 16
CONV_WIN = CHUNK + 2 * CONV_HALO
ROW_TILE = 512
VMEM_LIMIT = 56 * 1024 * 1024


def _dot(a, b):
    return jnp.dot(a, b, preferred_element_type=F32)


def _dot_nt(a, b):
    return lax.dot_general(a, b, (((1,), (1,)), ((), ())), preferred_element_type=F32)


def _dot_tn(a, b):
    return lax.dot_general(a, b, (((0,), (0,)), ((), ())), preferred_element_type=F32)


def _sigmoid(z):
    e = jnp.exp(-jnp.abs(z))
    return jnp.where(z >= 0, 1.0, e) / (1.0 + e)


def _log_sigmoid(z):
    return jnp.minimum(z, 0.0) - jnp.log1p(jnp.exp(-jnp.abs(z)))


def _rms_scale(x, gain):
    return x * lax.rsqrt(jnp.mean(x * x, axis=-1, keepdims=True) + NORM_EPS) * gain


def _split3(x):
    hi = x.astype(BF16)
    r1 = x - hi.astype(F32)
    mid = r1.astype(BF16)
    lo = (r1 - mid.astype(F32)).astype(BF16)
    return hi, mid, lo


def _tri_cumsum(tri, x):
    hi, mid, lo = _split3(x)
    return _dot(tri, hi) + _dot(tri, mid) + _dot(tri, lo)


def _params(semantics):
    return pltpu.CompilerParams(dimension_semantics=semantics, vmem_limit_bytes=VMEM_LIMIT)


def _lb_kernel(logit_ref, lb_ref, *, depth):
    for d in range(2):
        rows = [logit_ref[d * depth + l:d * depth + l + 1, :] for l in range(depth)]
        mx = functools.reduce(jnp.maximum, rows)
        es = [jnp.exp(r - mx) for r in rows]
        tot = functools.reduce(lambda a, b: a + b, es)
        c = None
        c0 = None
        for l in range(depth):
            p = es[l] / tot
            c = p if c is None else c + p
            if l == 0:
                c0 = c
            lb_ref[2 * l + d:2 * l + d + 1, :] = c - c0


def _lower_bounds(logits):
    _, depth, width = logits.shape
    out = pl.pallas_call(
        functools.partial(_lb_kernel, depth=depth),
        out_shape=jax.ShapeDtypeStruct((2 * depth, width), F32),
        name="lower_bounds",
    )(logits.astype(F32).reshape(2 * depth, width))
    return out.reshape(depth, 2, width)


def _mem_kv_kernel(mem_ref, g_ref, w_ref, kv_ref):
    memn = _rms_scale(mem_ref[...], g_ref[...]).astype(BF16)
    kv_ref[...] = _dot(memn, w_ref[...]).astype(BF16)


def _mem_kv(mem, norm_mem, w_xkv):
    b, m, d = mem.shape
    depth = norm_mem.shape[0]
    return pl.pallas_call(
        _mem_kv_kernel,
        grid=(depth, b),
        in_specs=[
            pl.BlockSpec((None, m, d), lambda l, i: (i, 0, 0)),
            pl.BlockSpec((None, 1, d), lambda l, i: (l, 0, 0)),
            pl.BlockSpec((None, d, 2 * d), lambda l, i: (l, 0, 0)),
        ],
        out_specs=pl.BlockSpec((None, None, m, 2 * d), lambda l, i: (l, i, 0, 0)),
        out_shape=jax.ShapeDtypeStruct((depth, b, m, 2 * d), BF16),
        compiler_params=_params(("arbitrary", "arbitrary")),
        name="mem_kv",
    )(mem, norm_mem.reshape(depth, 1, d), w_xkv)


def _in_proj_kernel(x_ref, g_ref, w_ref, b_ref, wg_ref, bg_ref, lb_ref,
                    hq_ref, hkf_ref, hkb_ref, hlff_ref, hlfb_ref, hv_ref, hg_ref,
                    mqk_ref, mv_ref, mo_ref, gate_ref):
    w = GROUP_WIDTH
    xn = _rms_scale(x_ref[...], g_ref[...]).astype(BF16)

    def proj(group):
        c0 = group * w
        return _dot(xn, w_ref[:, c0:c0 + w]) + b_ref[:, c0:c0 + w]

    z = proj(0)
    hq_ref[...] = (z * _sigmoid(z)).astype(BF16)

    def forget(z, lb, k_ref, lf_ref):
        e = jnp.exp(-jnp.abs(z))
        log_sig = jnp.minimum(z, 0.0) - jnp.log1p(e)
        a = jnp.log(lb)
        c = jnp.log1p(-lb) + log_sig
        lf_ref[...] = jnp.maximum(a, c) + jnp.log1p(jnp.exp(-jnp.abs(a - c)))
        k_ref[...] = ((1.0 - lb) * (jnp.where(z >= 0, e, 1.0) / (1.0 + e))).astype(BF16)

    forget(proj(1), lb_ref[0:1, :], hkf_ref, hlff_ref)
    forget(proj(2), lb_ref[1:2, :], hkb_ref, hlfb_ref)
    hv_ref[...] = proj(3).astype(BF16)
    z = proj(4)
    hg_ref[...] = (z * _sigmoid(z)).astype(BF16)
    mqk_ref[:, 0:w] = proj(5).astype(BF16)
    mqk_ref[:, w:2 * w] = proj(6).astype(BF16)
    mv_ref[...] = proj(7).astype(BF16)
    mo_ref[...] = _sigmoid(proj(8)).astype(BF16)
    zg = _dot(xn, wg_ref[...]) + bg_ref[...]
    kind = lax.broadcasted_iota(jnp.int32, zg.shape, 1) % HEAD_DIM
    gate_ref[...] = jnp.where(kind >= 2, _log_sigmoid(zg), zg)


def _in_proj(x, layer, norm_mix, w_main, b_main, w_gate, b_gate, lb):
    b, s, d = x.shape
    tm = min(ROW_TILE, s)
    w = GROUP_WIDTH
    grid = (b, s // tm)
    row = lambda i, j: (i, j, 0)
    fixed3 = lambda i, j: (layer, 0, 0)
    act = lambda width, dtype: jax.ShapeDtypeStruct((b, s, width), dtype)
    out_shapes = [act(w, BF16), act(w, BF16), act(w, BF16), act(w, F32), act(w, F32), act(w, BF16),
                  act(w, BF16), act(2 * w, BF16), act(w, BF16), act(w, BF16), act(w, F32)]
    out_specs = [pl.BlockSpec((None, tm, o.shape[-1]), row) for o in out_shapes]
    return pl.pallas_call(
        _in_proj_kernel,
        grid=grid,
        in_specs=[
            pl.BlockSpec((None, tm, d), row),
            pl.BlockSpec((None, 1, d), fixed3),
            pl.BlockSpec((None, d, MAIN_COLS), fixed3, pipeline_mode=pl.Buffered(1)),
            pl.BlockSpec((None, 1, MAIN_COLS), fixed3),
            pl.BlockSpec((None, d, w), fixed3, pipeline_mode=pl.Buffered(1)),
            pl.BlockSpec((None, 1, w), fixed3),
            pl.BlockSpec((None, 2, w), fixed3),
        ],
        out_specs=out_specs,
        out_shape=out_shapes,
        compiler_params=_params(("arbitrary", "arbitrary")),
        name="in_proj",
    )(x, norm_mix, w_main, b_main, w_gate, b_gate, lb)


def _hgrn_intra(q, k, cum, reverse):
    c = CHUNK
    half, quarter = c // 2, c // 4
    rowi = lax.broadcasted_iota(jnp.int32, (c, HEAD_DIM), 0)
    if not reverse:
        ref1 = cum[half - 1:half]
        q_side1 = rowi >= half
        ref2 = jnp.where(rowi < half, cum[quarter - 1:quarter], cum[half + quarter - 1:half + quarter])
        q_side2 = (rowi & quarter) != 0
    else:
        ref1 = cum[half:half + 1]
        q_side1 = rowi < half
        ref2 = jnp.where(rowi < half, cum[quarter:quarter + 1], cum[half + quarter:half + quarter + 1])
        q_side2 = (rowi & quarter) == 0

    def level(ref, q_side):
        x = cum - ref
        e = jnp.exp(jnp.where(q_side, x, -x))
        qs = jnp.where(q_side, q * e, 0.0).astype(BF16)
        ks = jnp.where(q_side, 0.0, k * e).astype(BF16)
        return _dot_nt(qs, ks)

    a1 = level(ref1, q_side1)
    a2 = level(ref2, q_side2)
    r64 = lax.broadcasted_iota(jnp.int32, (c, c), 0)
    c64 = lax.broadcasted_iota(jnp.int32, (c, c), 1)
    same_half = ((r64 ^ c64) & half) == 0
    a = a1 + jnp.where(same_half, a2, 0.0)

    ones = jnp.ones((HEAD_DIM, HEAD_DIM), BF16)
    lane = lax.broadcasted_iota(jnp.int32, (SUB, HEAD_DIM), 1)
    subl = lax.broadcasted_iota(jnp.int32, (SUB, HEAD_DIM), 0)
    blocks = []
    for blk in range(c // SUB):
        b0 = blk * SUB
        cb, qb, kb = cum[b0:b0 + SUB], q[b0:b0 + SUB], k[b0:b0 + SUB]
        prods = []
        for s in range(SUB):
            e = jnp.exp(jnp.minimum(cb - cb[s:s + 1], 0.0))
            prods.append((qb * e * kb[s:s + 1]).astype(BF16))
        summed = _dot(jnp.concatenate(prods, axis=0), ones)
        acc = jnp.zeros((SUB, HEAD_DIM), F32)
        for s in range(SUB):
            acc = jnp.where(lane == b0 + s, summed[s * SUB:(s + 1) * SUB], acc)
        keep = (lane >= b0 + subl) if reverse else (lane <= b0 + subl)
        blocks.append(jnp.where(keep, acc, 0.0))
    diag = jnp.concatenate(blocks, axis=0)
    return a + diag[:, 0:c]


def _hgrn_chunk(q, k, lf, v, st_ref, tri, reverse):
    c = CHUNK
    cum = _tri_cumsum(tri, lf)
    q_in = (q * jnp.exp(cum)).astype(BF16)
    st = st_ref[...]
    a = _hgrn_intra(q, k, cum, reverse)
    o = _dot_nt(q_in, st.astype(BF16)) + _dot(a.astype(BF16), v)
    last = cum[0:1] if reverse else cum[c - 1:c]
    k_dec = (k * jnp.exp(last - cum)).astype(BF16)
    st_ref[...] = st * jnp.exp(last) + _dot_tn(v, k_dec)
    return o


def _hgrn_kernel(q_ref, kf_ref, kb_ref, lff_ref, lfb_ref, v_ref, g_ref, gain_ref, out_ref,
                 o_scr, stf_scr, stb_scr, *, seq):
    c = CHUNK
    nc = seq // c
    r = lax.broadcasted_iota(jnp.int32, (c, c), 0)
    s = lax.broadcasted_iota(jnp.int32, (c, c), 1)
    tri_f = (s <= r).astype(BF16)
    tri_b = (s >= r).astype(BF16)
    stf_scr[...] = jnp.zeros_like(stf_scr)
    stb_scr[...] = jnp.zeros_like(stb_scr)

    def step(i, accumulate):
        for reverse in (False, True):
            ci = (nc - 1 - i) if reverse else i
            rows = pl.ds(pl.multiple_of(ci * c, c), c)
            k_ref, lf_ref = (kb_ref, lfb_ref) if reverse else (kf_ref, lff_ref)
            o = _hgrn_chunk(q_ref[rows, :].astype(F32), k_ref[rows, :].astype(F32), lf_ref[rows, :],
                            v_ref[rows, :], stb_scr if reverse else stf_scr,
                            tri_b if reverse else tri_f, reverse)
            if accumulate:
                o_scr[rows, :] += o
            else:
                o_scr[rows, :] = o

    lax.fori_loop(0, nc // 2, lambda i, carry: (step(i, False), carry)[1], 0)
    lax.fori_loop(nc // 2, nc, lambda i, carry: (step(i, True), carry)[1], 0)

    _head_norm_epilogue(o_scr, g_ref, gain_ref, out_ref, seq)


def _head_norm_epilogue(o_scr, gate_ref, gain_ref, out_ref, seq):
    blk = min(256, seq)
    ones = jnp.full((HEAD_DIM, HEAD_DIM), 1.0, BF16)

    def body(i, carry):
        rows = pl.ds(pl.multiple_of(i * blk, blk), blk)
        o = o_scr[rows, :]
        sq = o * o
        hi = sq.astype(BF16)
        lo = (sq - hi.astype(F32)).astype(BF16)
        ms = (_dot(hi, ones) + _dot(lo, ones)) * (1.0 / HEAD_DIM)
        y = o * lax.rsqrt(ms + NORM_EPS) * gain_ref[...] * gate_ref[rows, :].astype(F32)
        out_ref[rows, :] = y.astype(out_ref.dtype)
        return carry

    lax.fori_loop(0, seq // blk, body, 0)


def _hgrn_scan(hq, hkf, hkb, hlff, hlfb, hv, hg, gain, layer):
    b, s, _ = hq.shape
    hd = HEAD_DIM
    blk = lambda: pl.BlockSpec((None, s, hd), lambda i, h: (i, 0, h))
    return pl.pallas_call(
        functools.partial(_hgrn_kernel, seq=s),
        grid=(b, HEADS),
        in_specs=[blk(), blk(), blk(), blk(), blk(), blk(), blk(),
                  pl.BlockSpec((None, 1, hd), lambda i, h: (layer, 0, h))],
        out_specs=blk(),
        out_shape=jax.ShapeDtypeStruct((b, s, GROUP_WIDTH), BF16),
        scratch_shapes=[pltpu.VMEM((s, hd), F32), pltpu.VMEM((hd, hd), F32), pltpu.VMEM((hd, hd), F32)],
        compiler_params=_params(("arbitrary", "arbitrary")),
        name="hgrn_scan",
    )(hq, hkf, hkb, hlff, hlfb, hv, hg, gain)


def _mlstm_chunk(q, k, v_ext, gates, st_ref, m_ref, tri, reverse):
    c = CHUNK
    ig_lane, lf_lane = (1, 3) if reverse else (0, 2)
    cum = _tri_cumsum(tri, gates)
    gates_t = gates.T
    cum_t = cum.T
    ig_c, cum_c = gates[:, ig_lane:ig_lane + 1], cum[:, lf_lane:lf_lane + 1]
    ig_r, cum_r = gates_t[ig_lane:ig_lane + 1, 0:c], cum_t[lf_lane:lf_lane + 1, 0:c]
    m = m_ref[...]

    t_idx = lax.broadcasted_iota(jnp.int32, (c, c), 0)
    s_idx = lax.broadcasted_iota(jnp.int32, (c, c), 1)
    causal = (s_idx >= t_idx) if reverse else (s_idx <= t_idx)
    log_d = jnp.where(causal, cum_c - cum_r + ig_r, -jnp.inf)
    log_inter = cum_c + m
    m_t = jnp.maximum(jnp.max(log_d, axis=-1, keepdims=True), log_inter)
    d_mat = jnp.exp(log_d - m_t)
    w_inter = jnp.exp(log_inter - m_t)

    st = st_ref[...]
    scores = _dot_nt(q, k) * d_mat
    res = _dot(scores.astype(BF16), v_ext) + w_inter * _dot_nt(q, st.astype(BF16))
    num, den = res[:, 0:HEAD_DIM], res[:, HEAD_DIM:2 * HEAD_DIM]
    h_out = num / jnp.maximum(jnp.abs(den), jnp.exp(-m_t))

    last = cum_c[0:1] if reverse else cum_c[c - 1:c]
    log_w = last - cum_c + ig_c
    m_new = jnp.maximum(last + m, jnp.max(log_w, axis=0, keepdims=True))
    k_w = (k.astype(F32) * jnp.exp(log_w - m_new)).astype(BF16)
    st_ref[...] = jnp.exp(last + m - m_new) * st + _dot_tn(v_ext, k_w)
    m_ref[...] = m_new
    return h_out


def _mlstm_kernel(qk_q_ref, qk_k_ref, cwq_ref, cwk_ref, cbq_ref, cbk_ref, v_ref, og_ref, gates_ref,
                  gain_ref, out_ref, pad_scr, q_scr, k_scr, o_scr, stf_scr, stb_scr, mf_scr, mb_scr, *, seq):
    c = CHUNK
    nc = seq // c
    hd = HEAD_DIM
    halo = CONV_HALO

    pad_scr[0:halo, :] = jnp.zeros((halo, 2 * hd), BF16)
    pad_scr[halo + seq:2 * halo + seq, :] = jnp.zeros((halo, 2 * hd), BF16)
    pad_scr[halo:halo + seq, 0:hd] = qk_q_ref[...]
    pad_scr[halo:halo + seq, hd:2 * hd] = qk_k_ref[...]
    out_i = lax.broadcasted_iota(jnp.int32, (c, CONV_WIN), 0)
    win_i = lax.broadcasted_iota(jnp.int32, (c, CONV_WIN), 1)
    shifts = [(win_i == out_i + (j - CONV_WIDTH // 2) + halo).astype(BF16) for j in range(CONV_WIDTH)]
    taps = jnp.concatenate([cwq_ref[...], cwk_ref[...]], axis=1)
    bias = jnp.concatenate([cbq_ref[...], cbk_ref[...]], axis=1)

    def conv_body(i, carry):
        start = pl.multiple_of(i * c, c)
        win = pad_scr[pl.ds(start, CONV_WIN), :]
        y = bias
        for j in range(CONV_WIDTH):
            y = y + taps[j:j + 1, :] * _dot(shifts[j], win)
        y = y * _sigmoid(y)
        q_scr[pl.ds(start, c), :] = y[:, 0:hd].astype(BF16)
        k_scr[pl.ds(start, c), :] = (y[:, hd:2 * hd] * (hd ** -0.5)).astype(BF16)
        return carry

    lax.fori_loop(0, nc, conv_body, 0)

    r = lax.broadcasted_iota(jnp.int32, (c, c), 0)
    s = lax.broadcasted_iota(jnp.int32, (c, c), 1)
    tri_f = (s <= r).astype(BF16)
    tri_b = (s >= r).astype(BF16)
    stf_scr[...] = jnp.zeros_like(stf_scr)
    stb_scr[...] = jnp.zeros_like(stb_scr)
    mf_scr[...] = jnp.zeros_like(mf_scr)
    mb_scr[...] = jnp.zeros_like(mb_scr)
    ones = jnp.ones((c, hd), BF16)

    def step(i, accumulate):
        for reverse in (False, True):
            ci = (nc - 1 - i) if reverse else i
            rows = pl.ds(pl.multiple_of(ci * c, c), c)
            v_ext = jnp.concatenate([v_ref[rows, :], ones], axis=1)
            h = _mlstm_chunk(q_scr[rows, :], k_scr[rows, :], v_ext, gates_ref[rows, :],
                             stb_scr if reverse else stf_scr, mb_scr if reverse else mf_scr,
                             tri_b if reverse else tri_f, reverse)
            if accumulate:
                o_scr[rows, :] += h
            else:
                o_scr[rows, :] = h

    lax.fori_loop(0, nc // 2, lambda i, carry: (step(i, False), carry)[1], 0)
    lax.fori_loop(nc // 2, nc, lambda i, carry: (step(i, True), carry)[1], 0)

    _head_norm_epilogue(o_scr, og_ref, gain_ref, out_ref, seq)


def _mlstm_scan(mqk, conv_w, conv_b, mv, mo, gates, gain, layer):
    b, s, _ = mv.shape
    hd = HEAD_DIM
    blk = lambda off=0: pl.BlockSpec((None, s, hd), lambda i, h: (i, 0, h + off))
    par = lambda rows, off=0: pl.BlockSpec((None, rows, hd), lambda i, h: (layer, 0, h + off))
    return pl.pallas_call(
        functools.partial(_mlstm_kernel, seq=s),
        grid=(b, HEADS),
        in_specs=[blk(), blk(HEADS), par(CONV_WIDTH), par(CONV_WIDTH, HEADS), par(1), par(1, HEADS),
                  blk(), blk(), blk(), par(1)],
        out_specs=blk(),
        out_shape=jax.ShapeDtypeStruct((b, s, GROUP_WIDTH), BF16),
        scratch_shapes=[pltpu.VMEM((s + 2 * CONV_HALO, 2 * hd), BF16),
                        pltpu.VMEM((s, hd), BF16), pltpu.VMEM((s, hd), BF16), pltpu.VMEM((s, hd), F32),
                        pltpu.VMEM((2 * hd, hd), F32), pltpu.VMEM((2 * hd, hd), F32),
                        pltpu.VMEM((1, 1), F32), pltpu.VMEM((1, 1), F32)],
        compiler_params=_params(("arbitrary", "arbitrary")),
        name="mlstm_scan",
    )(mqk, mqk, conv_w, conv_w, conv_b, conv_b, mv, mo, gates, gain)


def _attn_kernel(x_ref, hmix_ref, mmix_ref, wout_ref, g_ref, wq_ref, kv_ref, wo_ref, out_ref):
    w = GROUP_WIDTH
    hd = XATTN_HEAD_DIM
    x1 = x_ref[...] + _dot(hmix_ref[...], wout_ref[0:w, :]) + _dot(mmix_ref[...], wout_ref[w:2 * w, :])
    xn = _rms_scale(x1, g_ref[...]).astype(BF16)
    q = (_dot(xn, wq_ref[...]) * (hd ** -0.5)).astype(BF16)
    heads = []
    for h in range(HEADS):
        kh = kv_ref[:, h * hd:(h + 1) * hd]
        vh = kv_ref[:, D_MODEL + h * hd:D_MODEL + (h + 1) * hd]
        sc = _dot_nt(q[:, h * hd:(h + 1) * hd], kh)
        p = jnp.exp(sc - jnp.max(sc, axis=-1, keepdims=True))
        o = _dot(p.astype(BF16), vh) / jnp.sum(p, axis=-1, keepdims=True)
        heads.append(o.astype(BF16))
    out_ref[...] = x1 + _dot(jnp.concatenate(heads, axis=1), wo_ref[...])


def _attn_block(x, hmix, mmix, w_out, norm_xattn, w_xq, kv, w_xo, layer):
    b, s, d = x.shape
    tm = min(ROW_TILE, s)
    m = kv.shape[2]
    row = lambda i, j: (i, j, 0)
    fixed3 = lambda i, j: (layer, 0, 0)
    wspec = lambda: pl.BlockSpec((None, d, d), fixed3, pipeline_mode=pl.Buffered(1))
    return pl.pallas_call(
        _attn_kernel,
        grid=(b, s // tm),
        in_specs=[
            pl.BlockSpec((None, tm, d), row),
            pl.BlockSpec((None, tm, GROUP_WIDTH), row),
            pl.BlockSpec((None, tm, GROUP_WIDTH), row),
            wspec(),
            pl.BlockSpec((None, 1, d), fixed3),
            wspec(),
            pl.BlockSpec((None, None, m, 2 * d), lambda i, j: (layer, i, 0, 0)),
            wspec(),
        ],
        out_specs=pl.BlockSpec((None, tm, d), row),
        out_shape=jax.ShapeDtypeStruct((b, s, d), F32),
        compiler_params=_params(("arbitrary", "arbitrary")),
        name="attn_block",
    )(x, hmix, mmix, w_out, norm_xattn, w_xq, kv, w_xo)


def _mlp_kernel(x_ref, g_ref, wup_ref, wdn_ref, gfin_ref, out_ref, *, final):
    x = x_ref[...]
    xn = _rms_scale(x, g_ref[...]).astype(BF16)
    acc = x
    step = D_MODEL
    for c0 in range(0, D_FF, step):
        hcol = jnp.maximum(_dot(xn, wup_ref[:, c0:c0 + step]), 0.0)
        acc = acc + _dot((hcol * hcol).astype(BF16), wdn_ref[c0:c0 + step, :])
    if final:
        acc = _rms_scale(acc, gfin_ref[...])
    out_ref[...] = acc


def _mlp_block(x, norm_mlp, w_up, w_down, norm_final, layer, final):
    b, s, d = x.shape
    tm = min(ROW_TILE, s)
    row = lambda i, j: (i, j, 0)
    fixed3 = lambda i, j: (layer, 0, 0)
    return pl.pallas_call(
        functools.partial(_mlp_kernel, final=final),
        grid=(b, s // tm),
        in_specs=[
            pl.BlockSpec((None, tm, d), row),
            pl.BlockSpec((None, 1, d), fixed3),
            pl.BlockSpec((None, d, D_FF), fixed3, pipeline_mode=pl.Buffered(1)),
            pl.BlockSpec((None, D_FF, d), fixed3, pipeline_mode=pl.Buffered(1)),
            pl.BlockSpec((1, d), lambda i, j: (0, 0)),
        ],
        out_specs=pl.BlockSpec((None, tm, d), row),
        out_shape=jax.ShapeDtypeStruct((b, s, d), F32),
        compiler_params=_params(("arbitrary", "arbitrary")),
        name="mlp_block",
    )(x, norm_mlp, w_up, w_down, norm_final)


def kernel(x, mem, norm_mix, norm_xattn, norm_mem, norm_mlp, norm_final, w_in, b_in, hgrn_lb_logits, hgrn_norm,
           mlstm_conv_w, mlstm_conv_b, mlstm_norm, w_out, w_xq, w_xkv, w_xo, w_up, w_down):
    depth = w_in.shape[0]
    b, s, d = x.shape
    assert d == D_MODEL and s % (2 * CHUNK) == 0 and s % min(ROW_TILE, s) == 0
    assert w_in.shape[-1] == MAIN_COLS + 4 * HEADS

    row3 = lambda t: t.reshape(depth, 1, t.shape[-1])
    w_gate = w_in[:, :, MAIN_COLS:].reshape(depth, d, 4, HEADS).transpose(0, 1, 3, 2)
    w_gate = jnp.pad(w_gate, ((0, 0), (0, 0), (0, 0), (0, HEAD_DIM - 4))).reshape(depth, d, GROUP_WIDTH)
    b_gate = b_in[:, MAIN_COLS:].reshape(depth, 4, HEADS).transpose(0, 2, 1)
    b_gate = jnp.pad(b_gate, ((0, 0), (0, 0), (0, HEAD_DIM - 4))).reshape(depth, 1, GROUP_WIDTH)
    w_main = w_in[:, :, :MAIN_COLS].astype(BF16)
    b_main = row3(b_in[:, :MAIN_COLS])
    w_gate = w_gate.astype(BF16)
    w_out_b, w_xq_b, w_xkv_b, w_xo_b = (t.astype(BF16) for t in (w_out, w_xq, w_xkv, w_xo))
    w_up_b, w_down_b = w_up.astype(BF16), w_down.astype(BF16)
    norm_mix3, norm_xattn3, norm_mlp3 = row3(norm_mix), row3(norm_xattn), row3(norm_mlp)
    hgrn_norm3, mlstm_norm3, conv_b3 = row3(hgrn_norm), row3(mlstm_norm), row3(mlstm_conv_b)
    norm_final2 = norm_final.reshape(1, d)

    lb = _lower_bounds(hgrn_lb_logits)
    kv = _mem_kv(mem, norm_mem, w_xkv_b)

    for l in range(depth):
        (hq, hkf, hkb, hlff, hlfb, hv, hg, mqk, mv, mo, gates) = _in_proj(
            x, l, norm_mix3, w_main, b_main, w_gate, b_gate, lb)
        hmix = _hgrn_scan(hq, hkf, hkb, hlff, hlfb, hv, hg, hgrn_norm3, l)
        mmix = _mlstm_scan(mqk, mlstm_conv_w, conv_b3, mv, mo, gates, mlstm_norm3, l)
        x = _attn_block(x, hmix, mmix, w_out_b, norm_xattn3, w_xq_b, kv, w_xo_b, l)
        x = _mlp_block(x, norm_mlp3, w_up_b, w_down_b, norm_final2, l, final=(l == depth - 1))
    return x
```

```python
import functools

import jax
import jax.numpy as jnp
from jax import lax
from jax.experimental import pallas as pl
from jax.experimental.pallas import tpu as pltpu

F32 = jnp.float32
BF16 = jnp.bfloat16

D_MODEL = 1024
HEADS = 4
HEAD_DIM = 128
GROUP_WIDTH = HEADS * HEAD_DIM
CONV_WIDTH = 5
XATTN_HEAD_DIM = D_MODEL // HEADS
D_FF = 4 * D_MODEL
CHUNK = 64
SUB = 8
GATE_ROWS = 8
UNROLL = 4
NORM_EPS = 1e-6
LOG2E = 1.4426950408889634
MAIN_COLS = 9 * GROUP_WIDTH
CONV_HALO = 16
CONV_WIN = CHUNK + 2 * CONV_HALO
ROW_TILE = 512
VMEM_LIMIT = 56 * 1024 * 1024


def _dot(a, b):
    return jnp.dot(a, b, preferred_element_type=F32)


def _dot_nt(a, b):
    return lax.dot_general(a, b, (((1,), (1,)), ((), ())), preferred_element_type=F32)


def _dot_tn(a, b):
    return lax.dot_general(a, b, (((0,), (0,)), ((), ())), preferred_element_type=F32)


def _sigmoid(z):
    e = jnp.exp(-jnp.abs(z))
    return jnp.where(z >= 0, 1.0, e) / (1.0 + e)


def _log_sigmoid(z):
    return jnp.minimum(z, 0.0) - jnp.log1p(jnp.exp(-jnp.abs(z)))


def _rms_scale(x, gain):
    return x * lax.rsqrt(jnp.mean(x * x, axis=-1, keepdims=True) + NORM_EPS) * gain


def _split3_rows(x):
    hi = x.astype(BF16)
    r1 = x - hi.astype(F32)
    mid = r1.astype(BF16)
    lo = (r1 - mid.astype(F32)).astype(BF16)
    return jnp.concatenate([hi, mid, lo], axis=0)


def _tri3(reverse):
    r = lax.broadcasted_iota(jnp.int32, (CHUNK, 3 * CHUNK), 0)
    s = lax.broadcasted_iota(jnp.int32, (CHUNK, 3 * CHUNK), 1) % CHUNK
    return ((s >= r) if reverse else (s <= r)).astype(BF16)


def _params(semantics):
    return pltpu.CompilerParams(dimension_semantics=semantics, vmem_limit_bytes=VMEM_LIMIT)


def _lb_kernel(logit_ref, lb_ref, *, depth):
    for d in range(2):
        rows = [logit_ref[d * depth + l:d * depth + l + 1, :] for l in range(depth)]
        mx = functools.reduce(jnp.maximum, rows)
        es = [jnp.exp(r - mx) for r in rows]
        tot = functools.reduce(lambda a, b: a + b, es)
        c = None
        c0 = None
        for l in range(depth):
            p = es[l] / tot
            c = p if c is None else c + p
            if l == 0:
                c0 = c
            lb_ref[2 * l + d:2 * l + d + 1, :] = c - c0


def _lower_bounds(logits):
    _, depth, width = logits.shape
    out = pl.pallas_call(
        functools.partial(_lb_kernel, depth=depth),
        out_shape=jax.ShapeDtypeStruct((2 * depth, width), F32),
        name="lower_bounds",
    )(logits.astype(F32).reshape(2 * depth, width))
    return out.reshape(depth, 2, width)


def _mem_kv_kernel(mem_ref, g_ref, w_ref, kv_ref):
    memn = _rms_scale(mem_ref[...], g_ref[...]).astype(BF16)
    kv_ref[...] = _dot(memn, w_ref[...]).astype(BF16)


def _mem_kv(mem, norm_mem, w_xkv):
    b, m, d = mem.shape
    depth = norm_mem.shape[0]
    return pl.pallas_call(
        _mem_kv_kernel,
        grid=(depth, b),
        in_specs=[
            pl.BlockSpec((None, m, d), lambda l, i: (i, 0, 0)),
            pl.BlockSpec((None, 1, d), lambda l, i: (l, 0, 0)),
            pl.BlockSpec((None, d, 2 * d), lambda l, i: (l, 0, 0)),
        ],
        out_specs=pl.BlockSpec((None, None, m, 2 * d), lambda l, i: (l, i, 0, 0)),
        out_shape=jax.ShapeDtypeStruct((depth, b, m, 2 * d), BF16),
        compiler_params=_params(("arbitrary", "arbitrary")),
        name="mem_kv",
    )(mem, norm_mem.reshape(depth, 1, d), w_xkv)


def _in_proj_kernel(x_ref, g_ref, w_ref, b_ref, wg_ref, bg_ref, lb_ref,
                    hq_ref, hkf_ref, hkb_ref, hcf_ref, hcb_ref, hv_ref, hg_ref,
                    mqk_ref, mv_ref, mo_ref, gate_ref):
    w = GROUP_WIDTH
    c = CHUNK
    n_chunks = x_ref.shape[0] // c
    xn = _rms_scale(x_ref[...], g_ref[...]).astype(BF16)
    tri_f, tri_b = _tri3(False), _tri3(True)

    def proj(group):
        c0 = group * w
        return _dot(xn, w_ref[:, c0:c0 + w]) + b_ref[:, c0:c0 + w]

    z = proj(0)
    hq_ref[...] = (z * _sigmoid(z)).astype(BF16)

    def forget(z, lb, tri, k_ref, cum_ref):
        e = jnp.exp(-jnp.abs(z))
        log_sig = jnp.minimum(z, 0.0) - jnp.log1p(e)
        a = jnp.log(lb)
        t = jnp.log1p(-lb) + log_sig
        lf2 = (jnp.maximum(a, t) + jnp.log1p(jnp.exp(-jnp.abs(a - t)))) * LOG2E
        k_ref[...] = ((1.0 - lb) * (jnp.where(z >= 0, e, 1.0) / (1.0 + e))).astype(BF16)
        for j in range(n_chunks):
            cum_ref[j * c:(j + 1) * c, :] = _dot(tri, _split3_rows(lf2[j * c:(j + 1) * c]))

    forget(proj(1), lb_ref[0:1, :], tri_f, hkf_ref, hcf_ref)
    forget(proj(2), lb_ref[1:2, :], tri_b, hkb_ref, hcb_ref)
    hv_ref[...] = proj(3).astype(BF16)
    z = proj(4)
    hg_ref[...] = (z * _sigmoid(z)).astype(BF16)
    mqk_ref[:, 0:w] = proj(5).astype(BF16)
    mqk_ref[:, w:2 * w] = proj(6).astype(BF16)
    mv_ref[...] = proj(7).astype(BF16)
    mo_ref[...] = _sigmoid(proj(8)).astype(BF16)
    tm = x_ref.shape[0]
    zg = _dot_nt(wg_ref[...], xn) + bg_ref[...]
    kind = lax.broadcasted_iota(jnp.int32, zg.shape, 0) % GATE_ROWS
    g2 = jnp.where(kind >= 2, _log_sigmoid(zg), zg) * LOG2E
    g3 = _split3_rows(g2)
    u = lax.broadcasted_iota(jnp.int32, (tm, tm), 0)
    s = lax.broadcasted_iota(jnp.int32, (tm, tm), 1)
    same_chunk = (u // c) == (s // c)

    def lane_cumsum(tri):
        parts = _dot(g3, tri.astype(BF16))
        n = zg.shape[0]
        return parts[0:n] + parts[n:2 * n] + parts[2 * n:3 * n]

    prefix = lane_cumsum(same_chunk & (u <= s))
    suffix = lane_cumsum(same_chunk & (u >= s))
    rows = jnp.where(kind == 2, prefix, jnp.where(kind == 3, suffix, g2))
    for j in range(n_chunks):
        for h in range(HEADS):
            gate_ref[j, h] = rows[h * GATE_ROWS:(h + 1) * GATE_ROWS, j * c:(j + 1) * c]


def _in_proj(x, layer, norm_mix, w_main, b_main, w_gate, b_gate, lb):
    b, s, d = x.shape
    tm = min(ROW_TILE, s)
    w = GROUP_WIDTH
    grid = (b, s // tm)
    row = lambda i, j: (i, j, 0)
    fixed3 = lambda i, j: (layer, 0, 0)
    act = lambda width, dtype: jax.ShapeDtypeStruct((b, s, width), dtype)
    out_shapes = [act(w, BF16), act(w, BF16), act(w, BF16), act(w, F32), act(w, F32), act(w, BF16),
                  act(w, BF16), act(2 * w, BF16), act(w, BF16), act(w, BF16)]
    out_specs = [pl.BlockSpec((None, tm, o.shape[-1]), row) for o in out_shapes]
    out_shapes.append(jax.ShapeDtypeStruct((b, s // CHUNK, HEADS, GATE_ROWS, CHUNK), F32))
    out_specs.append(pl.BlockSpec((None, tm // CHUNK, HEADS, GATE_ROWS, CHUNK), lambda i, j: (i, j, 0, 0, 0)))
    return pl.pallas_call(
        _in_proj_kernel,
        grid=grid,
        in_specs=[
            pl.BlockSpec((None, tm, d), row),
            pl.BlockSpec((None, 1, d), fixed3),
            pl.BlockSpec((None, d, MAIN_COLS), fixed3, pipeline_mode=pl.Buffered(1)),
            pl.BlockSpec((None, 1, MAIN_COLS), fixed3),
            pl.BlockSpec((None, HEADS * GATE_ROWS, d), fixed3),
            pl.BlockSpec((None, HEADS * GATE_ROWS, 1), fixed3),
            pl.BlockSpec((None, 2, w), fixed3),
        ],
        out_specs=out_specs,
        out_shape=out_shapes,
        compiler_params=_params(("arbitrary", "arbitrary")),
        name="in_proj",
    )(x, norm_mix, w_main, b_main, w_gate, b_gate, lb)


def _hgrn_offdiag(q, k, cum, reverse):
    c = CHUNK
    r64 = lax.broadcasted_iota(jnp.int32, (c, c), 0)
    c64 = lax.broadcasted_iota(jnp.int32, (c, c), 1)
    a = None
    width = c // 2
    while width >= SUB:
        span = 2 * width
        refs = []
        for base in range(0, c, span):
            r = base + width if reverse else base + width - 1
            refs.append(jnp.broadcast_to(cum[r:r + 1], (span, HEAD_DIM)))
        ref = refs[0] if len(refs) == 1 else jnp.concatenate(refs, axis=0)
        e = jnp.exp2(-jnp.abs(cum - ref))
        qe, ke = q * e, k * e
        zeros = jnp.zeros((width, HEAD_DIM), F32)
        qparts, kparts = [], []
        for base in range(0, c, span):
            lo, hi = slice(base, base + width), slice(base + width, base + span)
            if reverse:
                qparts += [qe[lo], zeros]
                kparts += [zeros, ke[hi]]
            else:
                qparts += [zeros, qe[hi]]
                kparts += [ke[lo], zeros]
        a_l = _dot_nt(jnp.concatenate(qparts, axis=0).astype(BF16), jnp.concatenate(kparts, axis=0).astype(BF16))
        if span < c:
            a_l = jnp.where((r64 ^ c64) < span, a_l, 0.0)
        a = a_l if a is None else a + a_l
        width //= 2
    return a


def _hgrn_diag_sums(q, cum, c2_ref):
    ones = jnp.ones((HEAD_DIM, HEAD_DIM), BF16)
    prods = []
    for b0 in range(0, CHUNK, SUB):
        cb, qb = cum[b0:b0 + SUB], q[b0:b0 + SUB]
        for s in range(SUB):
            c2s = c2_ref[b0 + s:b0 + s + 1, :]
            prods.append(qb * jnp.exp2(jnp.minimum(cb - c2s, 0.0)))
    return _dot(jnp.concatenate(prods, axis=0).astype(BF16), ones)


def _hgrn_diag_assemble(summed, reverse, lane_is):
    subl = lax.broadcasted_iota(jnp.int32, (SUB, HEAD_DIM), 0)
    lane = lax.broadcasted_iota(jnp.int32, (SUB, HEAD_DIM), 1)
    blocks = []
    for b0 in range(0, CHUNK, SUB):
        acc = jnp.zeros((SUB, HEAD_DIM), F32)
        for s in range(SUB):
            i = b0 + s
            acc = jnp.where(lane_is(i), summed[i * SUB:(i + 1) * SUB], acc)
        keep = (lane >= b0 + subl) if reverse else (lane <= b0 + subl)
        blocks.append(jnp.where(keep, acc, 0.0))
    return jnp.concatenate(blocks, axis=0)


def _hgrn_chunk_stages(q_ref, k_ref, cum_ref, v_ref, rows, c2_ref, st_ref, o_scr, reverse, accumulate, lane_is):
    c = CHUNK
    q, k = q_ref[rows, :].astype(F32), k_ref[rows, :].astype(F32)
    cum, v = cum_ref[rows, :], v_ref[rows, :]
    c2_ref[...] = cum - jnp.log(k) * LOG2E
    q_in = (q * jnp.exp2(cum)).astype(BF16)
    last = cum[0:1] if reverse else cum[c - 1:c]
    kv_new = _dot_tn(v, (k * jnp.exp2(last - cum)).astype(BF16))
    a_off = _hgrn_offdiag(q, k, cum, reverse)
    summed = _hgrn_diag_sums(q, cum, c2_ref)
    yield
    a = a_off + _hgrn_diag_assemble(summed, reverse, lane_is)[:, 0:c]
    o_intra = _dot(a.astype(BF16), v)
    yield
    st = st_ref[...]
    o_inter = _dot_nt(q_in, st.astype(BF16))
    st_ref[...] = st * jnp.exp2(last) + kv_new
    yield
    if accumulate:
        o_scr[rows, :] += o_inter + o_intra
    else:
        o_scr[rows, :] = o_inter + o_intra


def _run_interleaved(stages):
    alive = list(stages)
    while alive:
        still = []
        for g in alive:
            try:
                next(g)
                still.append(g)
            except StopIteration:
                pass
        alive = still


def _lane_masks():
    lane = lax.broadcasted_iota(jnp.int32, (SUB, HEAD_DIM), 1)
    cache = {}

    def lane_is(i):
        if i not in cache:
            cache[i] = lane == i
        return cache[i]

    return lane_is


def _scan_loops(nc, step):
    groups = nc // UNROLL
    lax.fori_loop(0, groups // 2, lambda i, carry: (step(i * UNROLL, False), carry)[1], 0)
    lax.fori_loop(groups // 2, groups, lambda i, carry: (step(i * UNROLL, True), carry)[1], 0)


def _chunk_index(nc, first, u, reverse):
    return (nc - 1 - first - u) if reverse else first + u


def _chunk_rows(nc, first, u, reverse):
    return pl.ds(pl.multiple_of(_chunk_index(nc, first, u, reverse) * CHUNK, CHUNK), CHUNK)


def _hgrn_kernel(q_ref, kf_ref, kb_ref, cf_ref, cb_ref, v_ref, g_ref, gain_ref, out_ref,
                 o_scr, c2_scr, stf_scr, stb_scr, *, seq):
    nc = seq // CHUNK
    stf_scr[...] = jnp.zeros_like(stf_scr)
    stb_scr[...] = jnp.zeros_like(stb_scr)

    def step(first, accumulate):
        lane_is = _lane_masks()
        _run_interleaved([
            _hgrn_chunk_stages(q_ref, kb_ref if reverse else kf_ref, cb_ref if reverse else cf_ref, v_ref,
                               _chunk_rows(nc, first, u, reverse), c2_scr.at[2 * u + int(reverse)],
                               stb_scr if reverse else stf_scr, o_scr, reverse, accumulate, lane_is)
            for u in range(UNROLL) for reverse in (False, True)])

    _scan_loops(nc, step)
    _head_norm_epilogue(o_scr, g_ref, gain_ref, out_ref, seq)


def _head_norm_epilogue(o_scr, gate_ref, gain_ref, out_ref, seq):
    blk = min(256, seq)
    ones = jnp.full((HEAD_DIM, HEAD_DIM), 1.0, BF16)

    def body(i, carry):
        rows = pl.ds(pl.multiple_of(i * blk, blk), blk)
        o = o_scr[rows, :]
        sq = o * o
        hi = sq.astype(BF16)
        lo = (sq - hi.astype(F32)).astype(BF16)
        ms = (_dot(hi, ones) + _dot(lo, ones)) * (1.0 / HEAD_DIM)
        y = o * lax.rsqrt(ms + NORM_EPS) * gain_ref[...] * gate_ref[rows, :].astype(F32)
        out_ref[rows, :] = y.astype(out_ref.dtype)
        return carry

    lax.fori_loop(0, seq // blk, body, 0)


def _hgrn_scan(hq, hkf, hkb, hcf, hcb, hv, hg, gain, layer):
    b, s, _ = hq.shape
    hd = HEAD_DIM
    blk = lambda: pl.BlockSpec((None, s, hd), lambda i, h: (i, 0, h))
    return pl.pallas_call(
        functools.partial(_hgrn_kernel, seq=s),
        grid=(b, HEADS),
        in_specs=[blk(), blk(), blk(), blk(), blk(), blk(), blk(),
                  pl.BlockSpec((None, 1, hd), lambda i, h: (layer, 0, h))],
        out_specs=blk(),
        out_shape=jax.ShapeDtypeStruct((b, s, GROUP_WIDTH), BF16),
        scratch_shapes=[pltpu.VMEM((s, hd), F32), pltpu.VMEM((2 * UNROLL, CHUNK, hd), F32),
                        pltpu.VMEM((hd, hd), F32), pltpu.VMEM((hd, hd), F32)],
        compiler_params=_params(("arbitrary", "arbitrary")),
        name="hgrn_scan",
    )(hq, hkf, hkb, hcf, hcb, hv, hg, gain)


def _mlstm_chunk_stages(q_scr, k_scr, v_ref, gates_ref, ci, st_ref, m_ref, o_scr, reverse, accumulate):
    c = CHUNK
    rows = pl.ds(pl.multiple_of(ci * c, c), c)
    ig_row, cum_row = (1, 3) if reverse else (0, 2)
    q, k = q_scr[rows, :], k_scr[rows, :]
    v_ext = jnp.concatenate([v_ref[rows, :], jnp.ones((c, HEAD_DIM), BF16)], axis=1)
    g_rows = gates_ref[ci]
    hi = g_rows.astype(BF16).astype(F32)
    mid = (g_rows - hi).astype(BF16).astype(F32)
    lo = g_rows - hi - mid
    g3 = jnp.concatenate([hi, mid, lo, jnp.zeros_like(lo)], axis=0).astype(BF16)
    part = lax.broadcasted_iota(jnp.int32, (4 * GATE_ROWS, HEAD_DIM), 0)
    pick = lambda r: ((part % GATE_ROWS == r) & (part < 3 * GATE_ROWS)).astype(BF16)
    ig_c, cum_c = _dot_tn(g3, pick(ig_row)), _dot_tn(g3, pick(cum_row))
    qk = _dot_nt(q, k)
    yield
    ig_r, cum_r = g_rows[ig_row:ig_row + 1, :], g_rows[cum_row:cum_row + 1, :]
    t_idx = lax.broadcasted_iota(jnp.int32, (c, c), 0)
    s_idx = lax.broadcasted_iota(jnp.int32, (c, c), 1)
    causal = (s_idx >= t_idx) if reverse else (s_idx <= t_idx)
    log_d = jnp.where(causal, cum_c[:, 0:c] - cum_r + ig_r, -jnp.inf)
    row_max = jnp.max(log_d, axis=-1, keepdims=True)
    last = cum_c[0:1] if reverse else cum_c[c - 1:c]
    log_w = last - cum_c + ig_c
    w_max = jnp.max(log_w, axis=0, keepdims=True)
    yield
    m = m_ref[...]
    log_inter = cum_c + m
    m_t = jnp.maximum(row_max, log_inter)
    w_inter = jnp.exp2(log_inter - m_t)
    scores = qk * jnp.exp2(log_d - m_t[:, 0:c])
    intra = _dot(scores.astype(BF16), v_ext)
    m_new = jnp.maximum(last + m, w_max)
    m_ref[...] = m_new
    carry = jnp.exp2(last + m - m_new)
    kv_new = _dot_tn(v_ext, (k.astype(F32) * jnp.exp2(log_w - m_new)).astype(BF16))
    yield
    st = st_ref[...]
    inter = _dot_nt(q, st.astype(BF16))
    st_ref[...] = carry * st + kv_new
    yield
    num = intra[:, 0:HEAD_DIM] + w_inter * inter[:, 0:HEAD_DIM]
    den = intra[:, HEAD_DIM:2 * HEAD_DIM] + w_inter * inter[:, HEAD_DIM:2 * HEAD_DIM]
    h_out = num / jnp.maximum(jnp.abs(den), jnp.exp2(-m_t))
    if accumulate:
        o_scr[rows, :] += h_out
    else:
        o_scr[rows, :] = h_out


def _mlstm_kernel(qk_q_ref, qk_k_ref, cwq_ref, cwk_ref, cbq_ref, cbk_ref, v_ref, og_ref, gates_ref,
                  gain_ref, out_ref, pad_scr, q_scr, k_scr, o_scr, stf_scr, stb_scr, mf_scr, mb_scr, *, seq):
    c = CHUNK
    nc = seq // c
    hd = HEAD_DIM
    halo = CONV_HALO

    pad_scr[0:halo, :] = jnp.zeros((halo, 2 * hd), BF16)
    pad_scr[halo + seq:2 * halo + seq, :] = jnp.zeros((halo, 2 * hd), BF16)
    pad_scr[halo:halo + seq, 0:hd] = qk_q_ref[...]
    pad_scr[halo:halo + seq, hd:2 * hd] = qk_k_ref[...]
    out_i = lax.broadcasted_iota(jnp.int32, (c, CONV_WIN), 0)
    win_i = lax.broadcasted_iota(jnp.int32, (c, CONV_WIN), 1)
    shifts = [(win_i == out_i + (j - CONV_WIDTH // 2) + halo).astype(BF16) for j in range(CONV_WIDTH)]
    taps = jnp.concatenate([cwq_ref[...], cwk_ref[...]], axis=1)
    bias = jnp.concatenate([cbq_ref[...], cbk_ref[...]], axis=1)

    def conv_body(i, carry):
        start = pl.multiple_of(i * c, c)
        win = pad_scr[pl.ds(start, CONV_WIN), :]
        y = bias
        for j in range(CONV_WIDTH):
            y = y + taps[j:j + 1, :] * _dot(shifts[j], win)
        y = y * _sigmoid(y)
        q_scr[pl.ds(start, c), :] = y[:, 0:hd].astype(BF16)
        k_scr[pl.ds(start, c), :] = (y[:, hd:2 * hd] * (hd ** -0.5)).astype(BF16)
        return carry

    lax.fori_loop(0, nc, conv_body, 0, unroll=2)

    stf_scr[...] = jnp.zeros_like(stf_scr)
    stb_scr[...] = jnp.zeros_like(stb_scr)
    mf_scr[...] = jnp.zeros_like(mf_scr)
    mb_scr[...] = jnp.zeros_like(mb_scr)

    def step(first, accumulate):
        _run_interleaved([
            _mlstm_chunk_stages(q_scr, k_scr, v_ref, gates_ref, _chunk_index(nc, first, u, reverse),
                                stb_scr if reverse else stf_scr, mb_scr if reverse else mf_scr,
                                o_scr, reverse, accumulate)
            for u in range(UNROLL) for reverse in (False, True)])

    _scan_loops(nc, step)
    _head_norm_epilogue(o_scr, og_ref, gain_ref, out_ref, seq)


def _mlstm_scan(mqk, conv_w, conv_b, mv, mo, gates, gain, layer):
    b, s, _ = mv.shape
    hd = HEAD_DIM
    blk = lambda off=0: pl.BlockSpec((None, s, hd), lambda i, h: (i, 0, h + off))
    par = lambda rows, off=0: pl.BlockSpec((None, rows, hd), lambda i, h: (layer, 0, h + off))
    return pl.pallas_call(
        functools.partial(_mlstm_kernel, seq=s),
        grid=(b, HEADS),
        in_specs=[blk(), blk(HEADS), par(CONV_WIDTH), par(CONV_WIDTH, HEADS), par(1), par(1, HEADS),
                  blk(), blk(),
                  pl.BlockSpec((None, s // CHUNK, None, GATE_ROWS, CHUNK), lambda i, h: (i, 0, h, 0, 0)),
                  par(1)],
        out_specs=blk(),
        out_shape=jax.ShapeDtypeStruct((b, s, GROUP_WIDTH), BF16),
        scratch_shapes=[pltpu.VMEM((s + 2 * CONV_HALO, 2 * hd), BF16),
                        pltpu.VMEM((s, hd), BF16), pltpu.VMEM((s, hd), BF16), pltpu.VMEM((s, hd), F32),
                        pltpu.VMEM((2 * hd, hd), F32), pltpu.VMEM((2 * hd, hd), F32),
                        pltpu.VMEM((1, hd), F32), pltpu.VMEM((1, hd), F32)],
        compiler_params=_params(("arbitrary", "arbitrary")),
        name="mlstm_scan",
    )(mqk, mqk, conv_w, conv_w, conv_b, conv_b, mv, mo, gates, gain)


def _attn_kernel(x_ref, hmix_ref, mmix_ref, wout_ref, g_ref, wq_ref, kv_ref, wo_ref, out_ref):
    w = GROUP_WIDTH
    hd = XATTN_HEAD_DIM
    x1 = x_ref[...] + _dot(hmix_ref[...], wout_ref[0:w, :]) + _dot(mmix_ref[...], wout_ref[w:2 * w, :])
    xn = _rms_scale(x1, g_ref[...]).astype(BF16)
    q = (_dot(xn, wq_ref[...]) * (hd ** -0.5)).astype(BF16)
    heads = []
    for h in range(HEADS):
        kh = kv_ref[:, h * hd:(h + 1) * hd]
        vh = kv_ref[:, D_MODEL + h * hd:D_MODEL + (h + 1) * hd]
        sc = _dot_nt(q[:, h * hd:(h + 1) * hd], kh)
        p = jnp.exp(sc - jnp.max(sc, axis=-1, keepdims=True))
        o = _dot(p.astype(BF16), vh) / jnp.sum(p, axis=-1, keepdims=True)
        heads.append(o.astype(BF16))
    out_ref[...] = x1 + _dot(jnp.concatenate(heads, axis=1), wo_ref[...])


def _attn_block(x, hmix, mmix, w_out, norm_xattn, w_xq, kv, w_xo, layer):
    b, s, d = x.shape
    tm = min(ROW_TILE, s)
    m = kv.shape[2]
    row = lambda i, j: (i, j, 0)
    fixed3 = lambda i, j: (layer, 0, 0)
    wspec = lambda: pl.BlockSpec((None, d, d), fixed3, pipeline_mode=pl.Buffered(1))
    return pl.pallas_call(
        _attn_kernel,
        grid=(b, s // tm),
        in_specs=[
            pl.BlockSpec((None, tm, d), row),
            pl.BlockSpec((None, tm, GROUP_WIDTH), row),
            pl.BlockSpec((None, tm, GROUP_WIDTH), row),
            wspec(),
            pl.BlockSpec((None, 1, d), fixed3),
            wspec(),
            pl.BlockSpec((None, None, m, 2 * d), lambda i, j: (layer, i, 0, 0)),
            wspec(),
        ],
        out_specs=pl.BlockSpec((None, tm, d), row),
        out_shape=jax.ShapeDtypeStruct((b, s, d), F32),
        compiler_params=_params(("arbitrary", "arbitrary")),
        name="attn_block",
    )(x, hmix, mmix, w_out, norm_xattn, w_xq, kv, w_xo)


def _mlp_kernel(x_ref, g_ref, wup_ref, wdn_ref, gfin_ref, out_ref, *, final):
    x = x_ref[...]
    xn = _rms_scale(x, g_ref[...]).astype(BF16)
    acc = x
    step = D_MODEL
    for c0 in range(0, D_FF, step):
        hcol = jnp.maximum(_dot(xn, wup_ref[:, c0:c0 + step]), 0.0)
        acc = acc + _dot((hcol * hcol).astype(BF16), wdn_ref[c0:c0 + step, :])
    if final:
        acc = _rms_scale(acc, gfin_ref[...])
    out_ref[...] = acc


def _mlp_block(x, norm_mlp, w_up, w_down, norm_final, layer, final):
    b, s, d = x.shape
    tm = min(ROW_TILE, s)
    row = lambda i, j: (i, j, 0)
    fixed3 = lambda i, j: (layer, 0, 0)
    return pl.pallas_call(
        functools.partial(_mlp_kernel, final=final),
        grid=(b, s // tm),
        in_specs=[
            pl.BlockSpec((None, tm, d), row),
            pl.BlockSpec((None, 1, d), fixed3),
            pl.BlockSpec((None, d, D_FF), fixed3, pipeline_mode=pl.Buffered(1)),
            pl.BlockSpec((None, D_FF, d), fixed3, pipeline_mode=pl.Buffered(1)),
            pl.BlockSpec((1, d), lambda i, j: (0, 0)),
        ],
        out_specs=pl.BlockSpec((None, tm, d), row),
        out_shape=jax.ShapeDtypeStruct((b, s, d), F32),
        compiler_params=_params(("arbitrary", "arbitrary")),
        name="mlp_block",
    )(x, norm_mlp, w_up, w_down, norm_final)


def kernel(x, mem, norm_mix, norm_xattn, norm_mem, norm_mlp, norm_final, w_in, b_in, hgrn_lb_logits, hgrn_norm,
           mlstm_conv_w, mlstm_conv_b, mlstm_norm, w_out, w_xq, w_xkv, w_xo, w_up, w_down):
    depth = w_in.shape[0]
    b, s, d = x.shape
    assert d == D_MODEL and s % (2 * UNROLL * CHUNK) == 0 and s % min(ROW_TILE, s) == 0
    assert w_in.shape[-1] == MAIN_COLS + 4 * HEADS

    row3 = lambda t: t.reshape(depth, 1, t.shape[-1])
    w_gate = w_in[:, :, MAIN_COLS:].reshape(depth, d, 4, HEADS).transpose(0, 3, 2, 1)
    w_gate = jnp.pad(w_gate, ((0, 0), (0, 0), (0, GATE_ROWS - 4), (0, 0))).reshape(depth, HEADS * GATE_ROWS, d)
    b_gate = b_in[:, MAIN_COLS:].reshape(depth, 4, HEADS).transpose(0, 2, 1)
    b_gate = jnp.pad(b_gate, ((0, 0), (0, 0), (0, GATE_ROWS - 4))).reshape(depth, HEADS * GATE_ROWS, 1)
    w_main = w_in[:, :, :MAIN_COLS].astype(BF16)
    b_main = row3(b_in[:, :MAIN_COLS])
    w_gate = w_gate.astype(BF16)
    w_out_b, w_xq_b, w_xkv_b, w_xo_b = (t.astype(BF16) for t in (w_out, w_xq, w_xkv, w_xo))
    w_up_b, w_down_b = w_up.astype(BF16), w_down.astype(BF16)
    norm_mix3, norm_xattn3, norm_mlp3 = row3(norm_mix), row3(norm_xattn), row3(norm_mlp)
    hgrn_norm3, mlstm_norm3, conv_b3 = row3(hgrn_norm), row3(mlstm_norm), row3(mlstm_conv_b)
    norm_final2 = norm_final.reshape(1, d)

    lb = _lower_bounds(hgrn_lb_logits)
    kv = _mem_kv(mem, norm_mem, w_xkv_b)

    for l in range(depth):
        (hq, hkf, hkb, hcf, hcb, hv, hg, mqk, mv, mo, gates) = _in_proj(
            x, l, norm_mix3, w_main, b_main, w_gate, b_gate, lb)
        hmix = _hgrn_scan(hq, hkf, hkb, hcf, hcb, hv, hg, hgrn_norm3, l)
        mmix = _mlstm_scan(mqk, mlstm_conv_w, conv_b3, mv, mo, gates, mlstm_norm3, l)
        x = _attn_block(x, hmix, mmix, w_out_b, norm_xattn3, w_xq_b, kv, w_xo_b, l)
        x = _mlp_block(x, norm_mlp3, w_up_b, w_down_b, norm_final2, l, final=(l == depth - 1))
    return x
```

```python
import functools

import jax
import jax.numpy as jnp
from jax import lax
from jax.experimental import pallas as pl
from jax.experimental.pallas import tpu as pltpu

F32 = jnp.float32
BF16 = jnp.bfloat16

D_MODEL = 1024
HEADS = 4
HEAD_DIM = 128
GROUP_WIDTH = HEADS * HEAD_DIM
CONV_WIDTH = 5
XATTN_HEAD_DIM = D_MODEL // HEADS
D_FF = 4 * D_MODEL
CHUNK = 64
SUB = 8
GATE_ROWS = 8
UNROLL = 4
NORM_EPS = 1e-6
LOG2E = 1.4426950408889634
MAIN_COLS = 9 * GROUP_WIDTH
CONV_HALO = 16
CONV_WIN = CHUNK + 2 * CONV_HALO
ROW_TILE = 512
VMEM_LIMIT = 56 * 1024 * 1024


def _dot(a, b):
    return jnp.dot(a, b, preferred_element_type=F32)


def _dot_nt(a, b):
    return lax.dot_general(a, b, (((1,), (1,)), ((), ())), preferred_element_type=F32)


def _dot_tn(a, b):
    return lax.dot_general(a, b, (((0,), (0,)), ((), ())), preferred_element_type=F32)


def _sigmoid(z):
    e = jnp.exp(-jnp.abs(z))
    return jnp.where(z >= 0, 1.0, e) / (1.0 + e)


def _log_sigmoid(z):
    return jnp.minimum(z, 0.0) - jnp.log(1.0 + jnp.exp(-jnp.abs(z)))


def _rms_scale(x, gain):
    return x * lax.rsqrt(jnp.mean(x * x, axis=-1, keepdims=True) + NORM_EPS) * gain


def _split3_rows(x):
    hi = x.astype(BF16)
    r1 = x - hi.astype(F32)
    mid = r1.astype(BF16)
    lo = (r1 - mid.astype(F32)).astype(BF16)
    return jnp.concatenate([hi, mid, lo], axis=0)


def _tri3(reverse):
    r = lax.broadcasted_iota(jnp.int32, (CHUNK, 3 * CHUNK), 0)
    s = lax.broadcasted_iota(jnp.int32, (CHUNK, 3 * CHUNK), 1) % CHUNK
    return ((s >= r) if reverse else (s <= r)).astype(BF16)


def _params(semantics):
    return pltpu.CompilerParams(dimension_semantics=semantics, vmem_limit_bytes=VMEM_LIMIT)


def _lb_kernel(logit_ref, lb_ref, *, depth):
    for d in range(2):
        rows = [logit_ref[d * depth + l:d * depth + l + 1, :] for l in range(depth)]
        mx = functools.reduce(jnp.maximum, rows)
        es = [jnp.exp(r - mx) for r in rows]
        tot = functools.reduce(lambda a, b: a + b, es)
        c = None
        c0 = None
        for l in range(depth):
            p = es[l] / tot
            c = p if c is None else c + p
            if l == 0:
                c0 = c
            lb_ref[2 * l + d:2 * l + d + 1, :] = c - c0


def _lower_bounds(logits):
    _, depth, width = logits.shape
    out = pl.pallas_call(
        functools.partial(_lb_kernel, depth=depth),
        out_shape=jax.ShapeDtypeStruct((2 * depth, width), F32),
        name="lower_bounds",
    )(logits.astype(F32).reshape(2 * depth, width))
    return out.reshape(depth, 2, width)


def _mem_kv_kernel(mem_ref, g_ref, w_ref, kv_ref):
    memn = _rms_scale(mem_ref[...], g_ref[...]).astype(BF16)
    kv_ref[...] = _dot(memn, w_ref[...]).astype(BF16)


def _mem_kv(mem, norm_mem, w_xkv):
    b, m, d = mem.shape
    depth = norm_mem.shape[0]
    return pl.pallas_call(
        _mem_kv_kernel,
        grid=(depth, b),
        in_specs=[
            pl.BlockSpec((None, m, d), lambda l, i: (i, 0, 0)),
            pl.BlockSpec((None, 1, d), lambda l, i: (l, 0, 0)),
            pl.BlockSpec((None, d, 2 * d), lambda l, i: (l, 0, 0)),
        ],
        out_specs=pl.BlockSpec((None, None, m, 2 * d), lambda l, i: (l, i, 0, 0)),
        out_shape=jax.ShapeDtypeStruct((depth, b, m, 2 * d), BF16),
        compiler_params=_params(("arbitrary", "arbitrary")),
        name="mem_kv",
    )(mem, norm_mem.reshape(depth, 1, d), w_xkv)


def _in_proj_kernel(x_ref, g_ref, w_ref, b_ref, wg_ref, bg_ref, lb_ref,
                    hq_ref, hkf_ref, hkb_ref, hcf_ref, hcb_ref, hv_ref, hg_ref,
                    mqk_ref, mv_ref, mo_ref, gate_ref):
    w = GROUP_WIDTH
    c = CHUNK
    n_chunks = x_ref.shape[0] // c
    xn = _rms_scale(x_ref[...], g_ref[...]).astype(BF16)
    tri_f, tri_b = _tri3(False), _tri3(True)

    def proj(group):
        c0 = group * w
        return _dot(xn, w_ref[:, c0:c0 + w]) + b_ref[:, c0:c0 + w]

    z = proj(0)
    hq_ref[...] = (z * _sigmoid(z)).astype(BF16)

    def forget(z, lb, tri, k_ref, cum_ref):
        e = jnp.exp(-jnp.abs(z))
        log_sig = jnp.minimum(z, 0.0) - jnp.log(1.0 + e)
        a = jnp.log(lb)
        t = jnp.log1p(-lb) + log_sig
        lf2 = (jnp.maximum(a, t) + jnp.log(1.0 + jnp.exp(-jnp.abs(a - t)))) * LOG2E
        k_ref[...] = ((1.0 - lb) * (jnp.where(z >= 0, e, 1.0) / (1.0 + e))).astype(BF16)
        for j in range(n_chunks):
            cum_ref[j * c:(j + 1) * c, :] = _dot(tri, _split3_rows(lf2[j * c:(j + 1) * c]))

    forget(proj(1), lb_ref[0:1, :], tri_f, hkf_ref, hcf_ref)
    forget(proj(2), lb_ref[1:2, :], tri_b, hkb_ref, hcb_ref)
    hv_ref[...] = proj(3).astype(BF16)
    z = proj(4)
    hg_ref[...] = (z * _sigmoid(z)).astype(BF16)
    mqk_ref[:, 0:w] = proj(5).astype(BF16)
    mqk_ref[:, w:2 * w] = proj(6).astype(BF16)
    mv_ref[...] = proj(7).astype(BF16)
    mo_ref[...] = _sigmoid(proj(8)).astype(BF16)
    tm = x_ref.shape[0]
    zg = _dot_nt(wg_ref[...], xn) + bg_ref[...]
    kind = lax.broadcasted_iota(jnp.int32, zg.shape, 0) % GATE_ROWS
    g2 = jnp.where(kind >= 2, _log_sigmoid(zg), zg) * LOG2E
    g3 = _split3_rows(g2)
    u = lax.broadcasted_iota(jnp.int32, (tm, tm), 0)
    s = lax.broadcasted_iota(jnp.int32, (tm, tm), 1)
    same_chunk = (u // c) == (s // c)

    def lane_cumsum(tri):
        parts = _dot(g3, tri.astype(BF16))
        n = zg.shape[0]
        return parts[0:n] + parts[n:2 * n] + parts[2 * n:3 * n]

    prefix = lane_cumsum(same_chunk & (u <= s))
    suffix = lane_cumsum(same_chunk & (u >= s))
    rows = jnp.where(kind == 2, prefix, jnp.where(kind == 3, suffix, g2))
    for j in range(n_chunks):
        for h in range(HEADS):
            gate_ref[j, h] = rows[h * GATE_ROWS:(h + 1) * GATE_ROWS, j * c:(j + 1) * c]


def _in_proj(x, layer, norm_mix, w_main, b_main, w_gate, b_gate, lb):
    b, s, d = x.shape
    tm = min(ROW_TILE, s)
    w = GROUP_WIDTH
    grid = (b, s // tm)
    row = lambda i, j: (i, j, 0)
    fixed3 = lambda i, j: (layer, 0, 0)
    act = lambda width, dtype: jax.ShapeDtypeStruct((b, s, width), dtype)
    out_shapes = [act(w, BF16), act(w, BF16), act(w, BF16), act(w, F32), act(w, F32), act(w, BF16),
                  act(w, BF16), act(2 * w, BF16), act(w, BF16), act(w, BF16)]
    out_specs = [pl.BlockSpec((None, tm, o.shape[-1]), row) for o in out_shapes]
    out_shapes.append(jax.ShapeDtypeStruct((b, s // CHUNK, HEADS, GATE_ROWS, CHUNK), F32))
    out_specs.append(pl.BlockSpec((None, tm // CHUNK, HEADS, GATE_ROWS, CHUNK), lambda i, j: (i, j, 0, 0, 0)))
    return pl.pallas_call(
        _in_proj_kernel,
        grid=grid,
        in_specs=[
            pl.BlockSpec((None, tm, d), row),
            pl.BlockSpec((None, 1, d), fixed3),
            pl.BlockSpec((None, d, MAIN_COLS), fixed3, pipeline_mode=pl.Buffered(1)),
            pl.BlockSpec((None, 1, MAIN_COLS), fixed3),
            pl.BlockSpec((None, HEADS * GATE_ROWS, d), fixed3),
            pl.BlockSpec((None, HEADS * GATE_ROWS, 1), fixed3),
            pl.BlockSpec((None, 2, w), fixed3),
        ],
        out_specs=out_specs,
        out_shape=out_shapes,
        compiler_params=_params(("arbitrary", "arbitrary")),
        name="in_proj",
    )(x, norm_mix, w_main, b_main, w_gate, b_gate, lb)


def _hgrn_offdiag(q, k, cum, reverse):
    c = CHUNK
    r64 = lax.broadcasted_iota(jnp.int32, (c, c), 0)
    c64 = lax.broadcasted_iota(jnp.int32, (c, c), 1)
    a = None
    width = c // 2
    while width >= SUB:
        span = 2 * width
        expo = []
        for base in range(0, c, span):
            r = base + width if reverse else base + width - 1
            ref, lo, hi = cum[r:r + 1], cum[base:base + width], cum[base + width:base + span]
            expo += [lo - ref, ref - hi] if reverse else [ref - lo, hi - ref]
        e = jnp.exp2(jnp.concatenate(expo, axis=0))
        qe, ke = q * e, k * e
        zeros = jnp.zeros((width, HEAD_DIM), F32)
        qparts, kparts = [], []
        for base in range(0, c, span):
            lo, hi = slice(base, base + width), slice(base + width, base + span)
            if reverse:
                qparts += [qe[lo], zeros]
                kparts += [zeros, ke[hi]]
            else:
                qparts += [zeros, qe[hi]]
                kparts += [ke[lo], zeros]
        a_l = _dot_nt(jnp.concatenate(qparts, axis=0).astype(BF16), jnp.concatenate(kparts, axis=0).astype(BF16))
        if span < c:
            a_l = jnp.where((r64 ^ c64) < span, a_l, 0.0)
        a = a_l if a is None else a + a_l
        width //= 2
    return a


def _hgrn_diag_sums(q, cum, c2_ref):
    ones = jnp.ones((HEAD_DIM, HEAD_DIM), BF16)
    prods = []
    for b0 in range(0, CHUNK, SUB):
        cb, qb = cum[b0:b0 + SUB], q[b0:b0 + SUB]
        for s in range(SUB):
            c2s = c2_ref[b0 + s:b0 + s + 1, :]
            prods.append(qb * jnp.exp2(jnp.minimum(cb - c2s, 0.0)))
    return _dot(jnp.concatenate(prods, axis=0).astype(BF16), ones)


def _hgrn_diag_assemble(summed, reverse, lane_is):
    subl = lax.broadcasted_iota(jnp.int32, (SUB, HEAD_DIM), 0)
    lane = lax.broadcasted_iota(jnp.int32, (SUB, HEAD_DIM), 1)
    blocks = []
    for b0 in range(0, CHUNK, SUB):
        acc = jnp.zeros((SUB, HEAD_DIM), F32)
        for s in range(SUB):
            i = b0 + s
            acc = jnp.where(lane_is(i), summed[i * SUB:(i + 1) * SUB], acc)
        keep = (lane >= b0 + subl) if reverse else (lane <= b0 + subl)
        blocks.append(jnp.where(keep, acc, 0.0))
    return jnp.concatenate(blocks, axis=0)


def _finish_rows(rows, o_scr, gate_ref, gain_ref, out_ref):
    total = o_scr[rows, :]
    sq = _dot((total * total).astype(BF16), jnp.ones((HEAD_DIM, HEAD_DIM), BF16))
    yield
    y = total * lax.rsqrt(sq * (1.0 / HEAD_DIM) + NORM_EPS) * gain_ref[...] * gate_ref[rows, :].astype(F32)
    out_ref[rows, :] = y.astype(out_ref.dtype)


def _hgrn_chunk_stages(q_ref, k_ref, cum_ref, v_ref, rows, c2_ref, st_ref, o_scr, reverse, accumulate, lane_is):
    c = CHUNK
    q, k = q_ref[rows, :].astype(F32), k_ref[rows, :].astype(F32)
    cum, v = cum_ref[rows, :], v_ref[rows, :]
    c2_ref[...] = cum - jnp.log(k) * LOG2E
    q_in = (q * jnp.exp2(cum)).astype(BF16)
    last = cum[0:1] if reverse else cum[c - 1:c]
    kv_new = _dot_tn(v, (k * jnp.exp2(last - cum)).astype(BF16))
    a_off = _hgrn_offdiag(q, k, cum, reverse)
    summed = _hgrn_diag_sums(q, cum, c2_ref)
    yield
    a = a_off + _hgrn_diag_assemble(summed, reverse, lane_is)[:, 0:c]
    o_intra = _dot(a.astype(BF16), v)
    yield
    st = st_ref[...]
    o_inter = _dot_nt(q_in, st.astype(BF16))
    st_ref[...] = st * jnp.exp2(last) + kv_new
    yield
    if accumulate:
        o_scr[rows, :] += o_inter + o_intra
    else:
        o_scr[rows, :] = o_inter + o_intra


def _run_interleaved(stages):
    alive = list(stages)
    while alive:
        still = []
        for g in alive:
            try:
                next(g)
                still.append(g)
            except StopIteration:
                pass
        alive = still


def _lane_masks():
    lane = lax.broadcasted_iota(jnp.int32, (SUB, HEAD_DIM), 1)
    cache = {}

    def lane_is(i):
        if i not in cache:
            cache[i] = lane == i
        return cache[i]

    return lane_is


def _scan_loops(nc, step):
    groups = nc // UNROLL
    lax.fori_loop(0, groups // 2, lambda i, carry: (step(i * UNROLL, False), carry)[1], 0)
    lax.fori_loop(groups // 2, groups, lambda i, carry: (step(i * UNROLL, True), carry)[1], 0)


def _chunk_index(nc, first, u, reverse):
    return (nc - 1 - first - u) if reverse else first + u


def _chunk_rows(nc, first, u, reverse):
    start = _chunk_index(nc, first, u, reverse) * CHUNK
    return pl.ds(start if isinstance(start, int) else pl.multiple_of(start, CHUNK), CHUNK)


def _mlstm_chunk_stages(q_scr, k_scr, v_ref, gates_ref, ci, st_ref, m_ref, o_scr, reverse, accumulate):
    c = CHUNK
    rows = pl.ds(pl.multiple_of(ci * c, c), c)
    ig_row, cum_row = (1, 3) if reverse else (0, 2)
    q, k = q_scr[rows, :], k_scr[rows, :]
    v_ext = jnp.concatenate([v_ref[rows, :], jnp.ones((c, HEAD_DIM), BF16)], axis=1)
    g_rows = gates_ref[ci]
    hi = g_rows.astype(BF16).astype(F32)
    mid = (g_rows - hi).astype(BF16).astype(F32)
    lo = g_rows - hi - mid
    g3 = jnp.concatenate([hi, mid, lo, jnp.zeros_like(lo)], axis=0).astype(BF16)
    part = lax.broadcasted_iota(jnp.int32, (4 * GATE_ROWS, HEAD_DIM), 0)
    pick = lambda r: ((part % GATE_ROWS == r) & (part < 3 * GATE_ROWS)).astype(BF16)
    ig_c, cum_c = _dot_tn(g3, pick(ig_row)), _dot_tn(g3, pick(cum_row))
    qk = _dot_nt(q, k)
    yield
    ig_r, cum_r = g_rows[ig_row:ig_row + 1, :], g_rows[cum_row:cum_row + 1, :]
    t_idx = lax.broadcasted_iota(jnp.int32, (c, c), 0)
    s_idx = lax.broadcasted_iota(jnp.int32, (c, c), 1)
    causal = (s_idx >= t_idx) if reverse else (s_idx <= t_idx)
    log_d = jnp.where(causal, cum_c[:, 0:c] - cum_r + ig_r, -jnp.inf)
    row_max = jnp.max(log_d, axis=-1, keepdims=True)
    last = cum_c[0:1] if reverse else cum_c[c - 1:c]
    log_w = last - cum_c + ig_c
    w_max = jnp.max(log_w, axis=0, keepdims=True)
    yield
    m = m_ref[...]
    log_inter = cum_c + m
    m_t = jnp.maximum(row_max, log_inter)
    w_inter = jnp.exp2(log_inter - m_t)
    scores = qk * jnp.exp2(log_d - m_t[:, 0:c])
    intra = _dot(scores.astype(BF16), v_ext)
    m_new = jnp.maximum(last + m, w_max)
    m_ref[...] = m_new
    carry = jnp.exp2(last + m - m_new)
    kv_new = _dot_tn(v_ext, (k.astype(F32) * jnp.exp2(log_w - m_new)).astype(BF16))
    yield
    st = st_ref[...]
    inter = _dot_nt(q, st.astype(BF16))
    st_ref[...] = carry * st + kv_new
    yield
    num = intra[:, 0:HEAD_DIM] + w_inter * inter[:, 0:HEAD_DIM]
    den = intra[:, HEAD_DIM:2 * HEAD_DIM] + w_inter * inter[:, HEAD_DIM:2 * HEAD_DIM]
    h_out = num / jnp.maximum(jnp.abs(den), jnp.exp2(-m_t))
    if accumulate:
        o_scr[rows, :] += h_out
    else:
        o_scr[rows, :] = h_out


def _mlstm_conv(mq_ref, mk_ref, cwq_ref, cwk_ref, cbq_ref, cbk_ref, pad_scr, q_scr, k_scr, seq):
    c = CHUNK
    hd = HEAD_DIM
    halo = CONV_HALO
    pad_scr[0:halo, :] = jnp.zeros((halo, 2 * hd), BF16)
    pad_scr[halo + seq:2 * halo + seq, :] = jnp.zeros((halo, 2 * hd), BF16)
    pad_scr[halo:halo + seq, 0:hd] = mq_ref[...]
    pad_scr[halo:halo + seq, hd:2 * hd] = mk_ref[...]
    out_i = lax.broadcasted_iota(jnp.int32, (c, CONV_WIN), 0)
    win_i = lax.broadcasted_iota(jnp.int32, (c, CONV_WIN), 1)
    shifts = [(win_i == out_i + (j - CONV_WIDTH // 2) + halo).astype(BF16) for j in range(CONV_WIDTH)]
    taps = jnp.concatenate([cwq_ref[...], cwk_ref[...]], axis=1)
    bias = jnp.concatenate([cbq_ref[...], cbk_ref[...]], axis=1)

    def conv_body(i, carry):
        start = pl.multiple_of(i * c, c)
        win = pad_scr[pl.ds(start, CONV_WIN), :]
        y = bias
        for j in range(CONV_WIDTH):
            y = y + taps[j:j + 1, :] * _dot(shifts[j], win)
        y = y * _sigmoid(y)
        q_scr[pl.ds(start, c), :] = y[:, 0:hd].astype(BF16)
        k_scr[pl.ds(start, c), :] = (y[:, hd:2 * hd] * (hd ** -0.5)).astype(BF16)
        return carry

    lax.fori_loop(0, seq // c, conv_body, 0, unroll=8)


def _mixer_kernel(hq_ref, hkf_ref, hkb_ref, hcf_ref, hcb_ref, hv_ref, hg_ref, hgain_ref,
                  mq_ref, mk_ref, cwq_ref, cwk_ref, cbq_ref, cbk_ref, mv_ref, mo_ref, gates_ref, mgain_ref,
                  hout_ref, mout_ref,
                  ho_scr, c2_scr, hstf_scr, hstb_scr,
                  pad_scr, q_scr, k_scr, mo_scr, mstf_scr, mstb_scr, mf_scr, mb_scr, *, seq):
    nc = seq // CHUNK
    _mlstm_conv(mq_ref, mk_ref, cwq_ref, cwk_ref, cbq_ref, cbk_ref, pad_scr, q_scr, k_scr, seq)
    for ref in (hstf_scr, hstb_scr, mstf_scr, mstb_scr, mf_scr, mb_scr):
        ref[...] = jnp.zeros_like(ref)

    def finish_stages(first):
        stages = []
        for u in range(UNROLL):
            for reverse in (False, True):
                rows = _chunk_rows(nc, first, u, reverse)
                stages.append(_finish_rows(rows, ho_scr, hg_ref, hgain_ref, hout_ref))
                stages.append(_finish_rows(rows, mo_scr, mo_ref, mgain_ref, mout_ref))
        return stages

    def step(first, accumulate):
        lane_is = _lane_masks()
        stages = finish_stages(first - UNROLL) if accumulate else []
        for u in range(UNROLL):
            for reverse in (False, True):
                stages.append(_hgrn_chunk_stages(
                    hq_ref, hkb_ref if reverse else hkf_ref, hcb_ref if reverse else hcf_ref, hv_ref,
                    _chunk_rows(nc, first, u, reverse), c2_scr.at[2 * u + int(reverse)],
                    hstb_scr if reverse else hstf_scr, ho_scr, reverse, accumulate, lane_is))
                stages.append(_mlstm_chunk_stages(
                    q_scr, k_scr, mv_ref, gates_ref, _chunk_index(nc, first, u, reverse),
                    mstb_scr if reverse else mstf_scr, mb_scr if reverse else mf_scr,
                    mo_scr, reverse, accumulate))
        _run_interleaved(stages)

    _scan_loops(nc, step)
    _run_interleaved(finish_stages(nc - UNROLL))


def _mixer_scan(hq, hkf, hkb, hcf, hcb, hv, hg, hgain, mqk, conv_w, conv_b, mv, mo, gates, mgain, layer):
    b, s, _ = hq.shape
    hd = HEAD_DIM
    blk = lambda off=0: pl.BlockSpec((None, s, hd), lambda i, h: (i, 0, h + off))
    par = lambda rows, off=0: pl.BlockSpec((None, rows, hd), lambda i, h: (layer, 0, h + off))
    out = jax.ShapeDtypeStruct((b, s, GROUP_WIDTH), BF16)
    return pl.pallas_call(
        functools.partial(_mixer_kernel, seq=s),
        grid=(b, HEADS),
        in_specs=[blk(), blk(), blk(), blk(), blk(), blk(), blk(), par(1),
                  blk(), blk(HEADS), par(CONV_WIDTH), par(CONV_WIDTH, HEADS), par(1), par(1, HEADS),
                  blk(), blk(),
                  pl.BlockSpec((None, s // CHUNK, None, GATE_ROWS, CHUNK), lambda i, h: (i, 0, h, 0, 0)),
                  par(1)],
        out_specs=[blk(), blk()],
        out_shape=[out, out],
        scratch_shapes=[pltpu.VMEM((s, hd), F32), pltpu.VMEM((2 * UNROLL, CHUNK, hd), F32),
                        pltpu.VMEM((hd, hd), F32), pltpu.VMEM((hd, hd), F32),
                        pltpu.VMEM((s + 2 * CONV_HALO, 2 * hd), BF16),
                        pltpu.VMEM((s, hd), BF16), pltpu.VMEM((s, hd), BF16), pltpu.VMEM((s, hd), F32),
                        pltpu.VMEM((2 * hd, hd), F32), pltpu.VMEM((2 * hd, hd), F32),
                        pltpu.VMEM((1, hd), F32), pltpu.VMEM((1, hd), F32)],
        compiler_params=_params(("arbitrary", "arbitrary")),
        name="mixer_scan",
    )(hq, hkf, hkb, hcf, hcb, hv, hg, hgain, mqk, mqk, conv_w, conv_w, conv_b, conv_b, mv, mo, gates, mgain)


def _attn_kernel(x_ref, hmix_ref, mmix_ref, wout_ref, g_ref, wq_ref, kv_ref, wo_ref, out_ref):
    w = GROUP_WIDTH
    hd = XATTN_HEAD_DIM
    x1 = x_ref[...] + _dot(hmix_ref[...], wout_ref[0:w, :]) + _dot(mmix_ref[...], wout_ref[w:2 * w, :])
    xn = _rms_scale(x1, g_ref[...]).astype(BF16)
    q = (_dot(xn, wq_ref[...]) * (hd ** -0.5)).astype(BF16)
    heads = []
    for h in range(HEADS):
        kh = kv_ref[:, h * hd:(h + 1) * hd]
        vh = kv_ref[:, D_MODEL + h * hd:D_MODEL + (h + 1) * hd]
        sc = _dot_nt(q[:, h * hd:(h + 1) * hd], kh)
        p = jnp.exp(sc - jnp.max(sc, axis=-1, keepdims=True))
        o = _dot(p.astype(BF16), vh) / jnp.sum(p, axis=-1, keepdims=True)
        heads.append(o.astype(BF16))
    out_ref[...] = x1 + _dot(jnp.concatenate(heads, axis=1), wo_ref[...])


def _attn_block(x, hmix, mmix, w_out, norm_xattn, w_xq, kv, w_xo, layer):
    b, s, d = x.shape
    tm = min(ROW_TILE, s)
    m = kv.shape[2]
    row = lambda i, j: (i, j, 0)
    fixed3 = lambda i, j: (layer, 0, 0)
    wspec = lambda: pl.BlockSpec((None, d, d), fixed3, pipeline_mode=pl.Buffered(1))
    return pl.pallas_call(
        _attn_kernel,
        grid=(b, s // tm),
        in_specs=[
            pl.BlockSpec((None, tm, d), row),
            pl.BlockSpec((None, tm, GROUP_WIDTH), row),
            pl.BlockSpec((None, tm, GROUP_WIDTH), row),
            wspec(),
            pl.BlockSpec((None, 1, d), fixed3),
            wspec(),
            pl.BlockSpec((None, None, m, 2 * d), lambda i, j: (layer, i, 0, 0)),
            wspec(),
        ],
        out_specs=pl.BlockSpec((None, tm, d), row),
        out_shape=jax.ShapeDtypeStruct((b, s, d), F32),
        compiler_params=_params(("arbitrary", "arbitrary")),
        name="attn_block",
    )(x, hmix, mmix, w_out, norm_xattn, w_xq, kv, w_xo)


def _mlp_kernel(x_ref, g_ref, wup_ref, wdn_ref, gfin_ref, out_ref, *, final):
    x = x_ref[...]
    xn = _rms_scale(x, g_ref[...]).astype(BF16)
    acc = x
    step = D_MODEL
    for c0 in range(0, D_FF, step):
        hcol = jnp.maximum(_dot(xn, wup_ref[:, c0:c0 + step]), 0.0)
        acc = acc + _dot((hcol * hcol).astype(BF16), wdn_ref[c0:c0 + step, :])
    if final:
        acc = _rms_scale(acc, gfin_ref[...])
    out_ref[...] = acc


def _mlp_block(x, norm_mlp, w_up, w_down, norm_final, layer, final):
    b, s, d = x.shape
    tm = min(ROW_TILE, s)
    row = lambda i, j: (i, j, 0)
    fixed3 = lambda i, j: (layer, 0, 0)
    return pl.pallas_call(
        functools.partial(_mlp_kernel, final=final),
        grid=(b, s // tm),
        in_specs=[
            pl.BlockSpec((None, tm, d), row),
            pl.BlockSpec((None, 1, d), fixed3),
            pl.BlockSpec((None, d, D_FF), fixed3, pipeline_mode=pl.Buffered(1)),
            pl.BlockSpec((None, D_FF, d), fixed3, pipeline_mode=pl.Buffered(1)),
            pl.BlockSpec((1, d), lambda i, j: (0, 0)),
        ],
        out_specs=pl.BlockSpec((None, tm, d), row),
        out_shape=jax.ShapeDtypeStruct((b, s, d), F32),
        compiler_params=_params(("arbitrary", "arbitrary")),
        name="mlp_block",
    )(x, norm_mlp, w_up, w_down, norm_final)


def kernel(x, mem, norm_mix, norm_xattn, norm_mem, norm_mlp, norm_final, w_in, b_in, hgrn_lb_logits, hgrn_norm,
           mlstm_conv_w, mlstm_conv_b, mlstm_norm, w_out, w_xq, w_xkv, w_xo, w_up, w_down):
    depth = w_in.shape[0]
    b, s, d = x.shape
    assert d == D_MODEL and s % (2 * UNROLL * CHUNK) == 0 and s % min(ROW_TILE, s) == 0
    assert w_in.shape[-1] == MAIN_COLS + 4 * HEADS

    row3 = lambda t: t.reshape(depth, 1, t.shape[-1])
    w_gate = w_in[:, :, MAIN_COLS:].reshape(depth, d, 4, HEADS).transpose(0, 3, 2, 1)
    w_gate = jnp.pad(w_gate, ((0, 0), (0, 0), (0, GATE_ROWS - 4), (0, 0))).reshape(depth, HEADS * GATE_ROWS, d)
    b_gate = b_in[:, MAIN_COLS:].reshape(depth, 4, HEADS).transpose(0, 2, 1)
    b_gate = jnp.pad(b_gate, ((0, 0), (0, 0), (0, GATE_ROWS - 4))).reshape(depth, HEADS * GATE_ROWS, 1)
    w_main = w_in[:, :, :MAIN_COLS].astype(BF16)
    b_main = row3(b_in[:, :MAIN_COLS])
    w_gate = w_gate.astype(BF16)
    w_out_b, w_xq_b, w_xkv_b, w_xo_b = (t.astype(BF16) for t in (w_out, w_xq, w_xkv, w_xo))
    w_up_b, w_down_b = w_up.astype(BF16), w_down.astype(BF16)
    norm_mix3, norm_xattn3, norm_mlp3 = row3(norm_mix), row3(norm_xattn), row3(norm_mlp)
    hgrn_norm3, mlstm_norm3, conv_b3 = row3(hgrn_norm), row3(mlstm_norm), row3(mlstm_conv_b)
    norm_final2 = norm_final.reshape(1, d)

    lb = _lower_bounds(hgrn_lb_logits)
    kv = _mem_kv(mem, norm_mem, w_xkv_b)

    for l in range(depth):
        (hq, hkf, hkb, hcf, hcb, hv, hg, mqk, mv, mo, gates) = _in_proj(
            x, l, norm_mix3, w_main, b_main, w_gate, b_gate, lb)
        hmix, mmix = _mixer_scan(hq, hkf, hkb, hcf, hcb, hv, hg, hgrn_norm3,
                                 mqk, mlstm_conv_w, conv_b3, mv, mo, gates, mlstm_norm3, l)
        x = _attn_block(x, hmix, mmix, w_out_b, norm_xattn3, w_xq_b, kv, w_xo_b, l)
        x = _mlp_block(x, norm_mlp3, w_up_b, w_down_b, norm_final2, l, final=(l == depth - 1))
    return x
```

```python
import functools

import jax
import jax.numpy as jnp
from jax import lax
from jax.experimental import pallas as pl
from jax.experimental.pallas import tpu as pltpu

F32 = jnp.float32
BF16 = jnp.bfloat16

D_MODEL = 1024
HEADS = 4
HEAD_DIM = 128
GROUP_WIDTH = HEADS * HEAD_DIM
CONV_WIDTH = 5
XATTN_HEAD_DIM = D_MODEL // HEADS
D_FF = 4 * D_MODEL
CHUNK = 64
SUB = 8
GATE_ROWS = 8
UNROLL = 4
NORM_EPS = 1e-6
LOG2E = 1.4426950408889634
MAIN_COLS = 9 * GROUP_WIDTH
CONV_HALO = 16
CONV_WIN = CHUNK + 2 * CONV_HALO
ROW_TILE = 512
VMEM_LIMIT = 56 * 1024 * 1024


def _dot(a, b):
    return jnp.dot(a, b, preferred_element_type=F32)


def _dot_nt(a, b):
    return lax.dot_general(a, b, (((1,), (1,)), ((), ())), preferred_element_type=F32)


def _dot_tn(a, b):
    return lax.dot_general(a, b, (((0,), (0,)), ((), ())), preferred_element_type=F32)


def _sigmoid(z):
    return 1.0 / (1.0 + jnp.exp(-z))


def _log_sigmoid(z):
    return jnp.minimum(z, 0.0) - jnp.log(1.0 + jnp.exp(-jnp.abs(z)))


def _rms_scale(x, gain):
    return x * lax.rsqrt(jnp.mean(x * x, axis=-1, keepdims=True) + NORM_EPS) * gain


def _split3_rows(x):
    hi = x.astype(BF16)
    r1 = x - hi.astype(F32)
    mid = r1.astype(BF16)
    lo = (r1 - mid.astype(F32)).astype(BF16)
    return jnp.concatenate([hi, mid, lo], axis=0)


def _tri3(reverse):
    r = lax.broadcasted_iota(jnp.int32, (CHUNK, 3 * CHUNK), 0)
    s = lax.broadcasted_iota(jnp.int32, (CHUNK, 3 * CHUNK), 1) % CHUNK
    return ((s >= r) if reverse else (s <= r)).astype(BF16)


def _params(semantics):
    return pltpu.CompilerParams(dimension_semantics=semantics, vmem_limit_bytes=VMEM_LIMIT)


def _lb_kernel(logit_ref, lb_ref, *, depth):
    for d in range(2):
        rows = [logit_ref[d * depth + l:d * depth + l + 1, :] for l in range(depth)]
        mx = functools.reduce(jnp.maximum, rows)
        es = [jnp.exp(r - mx) for r in rows]
        tot = functools.reduce(lambda a, b: a + b, es)
        c = None
        c0 = None
        for l in range(depth):
            p = es[l] / tot
            c = p if c is None else c + p
            if l == 0:
                c0 = c
            lb_ref[2 * l + d:2 * l + d + 1, :] = c - c0


def _lower_bounds(logits):
    _, depth, width = logits.shape
    out = pl.pallas_call(
        functools.partial(_lb_kernel, depth=depth),
        out_shape=jax.ShapeDtypeStruct((2 * depth, width), F32),
        name="lower_bounds",
    )(logits.astype(F32).reshape(2 * depth, width))
    return out.reshape(depth, 2, width)


def _mem_kv_kernel(mem_ref, g_ref, w_ref, kv_ref):
    memn = _rms_scale(mem_ref[...], g_ref[...]).astype(BF16)
    kv_ref[...] = _dot(memn, w_ref[...]).astype(BF16)


def _mem_kv(mem, norm_mem, w_xkv):
    b, m, d = mem.shape
    depth = norm_mem.shape[0]
    return pl.pallas_call(
        _mem_kv_kernel,
        grid=(depth, b),
        in_specs=[
            pl.BlockSpec((None, m, d), lambda l, i: (i, 0, 0)),
            pl.BlockSpec((None, 1, d), lambda l, i: (l, 0, 0)),
            pl.BlockSpec((None, d, 2 * d), lambda l, i: (l, 0, 0)),
        ],
        out_specs=pl.BlockSpec((None, None, m, 2 * d), lambda l, i: (l, i, 0, 0)),
        out_shape=jax.ShapeDtypeStruct((depth, b, m, 2 * d), BF16),
        compiler_params=_params(("arbitrary", "arbitrary")),
        name="mem_kv",
    )(mem, norm_mem.reshape(depth, 1, d), w_xkv)


def _in_proj_kernel(x_ref, g_ref, w_ref, b_ref, wg_ref, bg_ref, lb_ref,
                    hq_ref, hkf_ref, hkb_ref, hcf_ref, hcb_ref, hv_ref, hg_ref,
                    mqk_ref, mv_ref, mo_ref, gate_ref):
    w = GROUP_WIDTH
    c = CHUNK
    n_chunks = x_ref.shape[0] // c
    xn = _rms_scale(x_ref[...], g_ref[...]).astype(BF16)
    tri_f, tri_b = _tri3(False), _tri3(True)

    def proj(group):
        c0 = group * w
        return _dot(xn, w_ref[:, c0:c0 + w]) + b_ref[:, c0:c0 + w]

    def silu_to(z, ref):
        ref[...] = (z * _sigmoid(z)).astype(BF16)

    def forget_rows(z, lb, tri, k_ref, cum_ref, r0, r1):
        z = z[r0:r1]
        e = jnp.exp(-jnp.abs(z))
        log_sig = jnp.minimum(z, 0.0) - jnp.log(1.0 + e)
        a = jnp.log(lb)
        t = jnp.log1p(-lb) + log_sig
        lf2 = (jnp.maximum(a, t) + jnp.log(1.0 + jnp.exp(-jnp.abs(a - t)))) * LOG2E
        k_ref[r0:r1, :] = ((1.0 - lb) * (jnp.where(z >= 0, e, 1.0) / (1.0 + e))).astype(BF16)
        for j in range((r1 - r0) // c):
            cum_ref[r0 + j * c:r0 + (j + 1) * c, :] = _dot(tri, _split3_rows(lf2[j * c:(j + 1) * c]))

    half = x_ref.shape[0] // 2
    lb_f, lb_b = lb_ref[0:1, :], lb_ref[1:2, :]
    z_q, z_ff = proj(0), proj(1)
    silu_to(z_q, hq_ref)
    z_fb = proj(2)
    forget_rows(z_ff, lb_f, tri_f, hkf_ref, hcf_ref, 0, half)
    z_v = proj(3)
    forget_rows(z_ff, lb_f, tri_f, hkf_ref, hcf_ref, half, 2 * half)
    z_g = proj(4)
    forget_rows(z_fb, lb_b, tri_b, hkb_ref, hcb_ref, 0, half)
    z_mq = proj(5)
    forget_rows(z_fb, lb_b, tri_b, hkb_ref, hcb_ref, half, 2 * half)
    z_mk = proj(6)
    hv_ref[...] = z_v.astype(BF16)
    silu_to(z_g, hg_ref)
    z_mv = proj(7)
    mqk_ref[:, 0:w] = z_mq.astype(BF16)
    mqk_ref[:, w:2 * w] = z_mk.astype(BF16)
    z_mo = proj(8)
    mv_ref[...] = z_mv.astype(BF16)
    mo_ref[...] = _sigmoid(z_mo).astype(BF16)
    tm = x_ref.shape[0]
    zg = _dot_nt(wg_ref[...], xn) + bg_ref[...]
    kind = lax.broadcasted_iota(jnp.int32, zg.shape, 0) % GATE_ROWS
    g2 = jnp.where(kind >= 2, _log_sigmoid(zg), zg) * LOG2E
    g3 = _split3_rows(g2)
    u = lax.broadcasted_iota(jnp.int32, (tm, tm), 0)
    s = lax.broadcasted_iota(jnp.int32, (tm, tm), 1)
    same_chunk = (u // c) == (s // c)

    def lane_cumsum(tri):
        parts = _dot(g3, tri.astype(BF16))
        n = zg.shape[0]
        return parts[0:n] + parts[n:2 * n] + parts[2 * n:3 * n]

    prefix = lane_cumsum(same_chunk & (u <= s))
    suffix = lane_cumsum(same_chunk & (u >= s))
    rows = jnp.where(kind == 2, prefix, jnp.where(kind == 3, suffix, g2))
    for j in range(n_chunks):
        for h in range(HEADS):
            gate_ref[j, h] = rows[h * GATE_ROWS:(h + 1) * GATE_ROWS, j * c:(j + 1) * c]


def _in_proj(x, layer, norm_mix, w_main, b_main, w_gate, b_gate, lb):
    b, s, d = x.shape
    tm = min(ROW_TILE, s)
    w = GROUP_WIDTH
    grid = (b, s // tm)
    row = lambda i, j: (i, j, 0)
    fixed3 = lambda i, j: (layer, 0, 0)
    act = lambda width, dtype: jax.ShapeDtypeStruct((b, s, width), dtype)
    out_shapes = [act(w, BF16), act(w, BF16), act(w, BF16), act(w, F32), act(w, F32), act(w, BF16),
                  act(w, BF16), act(2 * w, BF16), act(w, BF16), act(w, BF16)]
    out_specs = [pl.BlockSpec((None, tm, o.shape[-1]), row) for o in out_shapes]
    out_shapes.append(jax.ShapeDtypeStruct((b, s // CHUNK, HEADS, GATE_ROWS, CHUNK), F32))
    out_specs.append(pl.BlockSpec((None, tm // CHUNK, HEADS, GATE_ROWS, CHUNK), lambda i, j: (i, j, 0, 0, 0)))
    return pl.pallas_call(
        _in_proj_kernel,
        grid=grid,
        in_specs=[
            pl.BlockSpec((None, tm, d), row),
            pl.BlockSpec((None, 1, d), fixed3),
            pl.BlockSpec((None, d, MAIN_COLS), fixed3, pipeline_mode=pl.Buffered(1)),
            pl.BlockSpec((None, 1, MAIN_COLS), fixed3),
            pl.BlockSpec((None, HEADS * GATE_ROWS, d), fixed3),
            pl.BlockSpec((None, HEADS * GATE_ROWS, 1), fixed3),
            pl.BlockSpec((None, 2, w), fixed3),
        ],
        out_specs=out_specs,
        out_shape=out_shapes,
        compiler_params=_params(("arbitrary", "arbitrary")),
        name="in_proj",
    )(x, norm_mix, w_main, b_main, w_gate, b_gate, lb)


def _hgrn_offdiag(q, k, cum, reverse):
    c = CHUNK
    r64 = lax.broadcasted_iota(jnp.int32, (c, c), 0)
    c64 = lax.broadcasted_iota(jnp.int32, (c, c), 1)
    a = None
    width = c // 2
    while width >= SUB:
        span = 2 * width
        expo = []
        for base in range(0, c, span):
            r = base + width if reverse else base + width - 1
            ref, lo, hi = cum[r:r + 1], cum[base:base + width], cum[base + width:base + span]
            expo += [lo - ref, ref - hi] if reverse else [ref - lo, hi - ref]
        e = jnp.exp2(jnp.concatenate(expo, axis=0))
        qe, ke = q * e, k * e
        zeros = jnp.zeros((width, HEAD_DIM), F32)
        qparts, kparts = [], []
        for base in range(0, c, span):
            lo, hi = slice(base, base + width), slice(base + width, base + span)
            if reverse:
                qparts += [qe[lo], zeros]
                kparts += [zeros, ke[hi]]
            else:
                qparts += [zeros, qe[hi]]
                kparts += [ke[lo], zeros]
        a_l = _dot_nt(jnp.concatenate(qparts, axis=0).astype(BF16), jnp.concatenate(kparts, axis=0).astype(BF16))
        if span < c:
            a_l = jnp.where((r64 ^ c64) < span, a_l, 0.0)
        a = a_l if a is None else a + a_l
        width //= 2
    return a


def _hgrn_diag_sums(q, cum, c2_ref):
    ones = jnp.ones((HEAD_DIM, HEAD_DIM), BF16)
    prods = []
    for b0 in range(0, CHUNK, SUB):
        cb, qb = cum[b0:b0 + SUB], q[b0:b0 + SUB]
        for s in range(SUB):
            c2s = c2_ref[b0 + s:b0 + s + 1, :]
            prods.append(qb * jnp.exp2(jnp.minimum(cb - c2s, 0.0)))
    return _dot(jnp.concatenate(prods, axis=0).astype(BF16), ones)


def _hgrn_diag_assemble(summed, reverse, lane_is):
    subl = lax.broadcasted_iota(jnp.int32, (SUB, HEAD_DIM), 0)
    lane = lax.broadcasted_iota(jnp.int32, (SUB, HEAD_DIM), 1)
    blocks = []
    for b0 in range(0, CHUNK, SUB):
        acc = jnp.zeros((SUB, HEAD_DIM), F32)
        for s in range(SUB):
            i = b0 + s
            acc = jnp.where(lane_is(i), summed[i * SUB:(i + 1) * SUB], acc)
        keep = (lane >= b0 + subl) if reverse else (lane <= b0 + subl)
        blocks.append(jnp.where(keep, acc, 0.0))
    return jnp.concatenate(blocks, axis=0)


def _finish_rows(rows, o_scr, gate_ref, gain_ref, out_ref):
    total = o_scr[rows, :]
    sq = _dot((total * total).astype(BF16), jnp.ones((HEAD_DIM, HEAD_DIM), BF16))
    yield
    y = total * lax.rsqrt(sq * (1.0 / HEAD_DIM) + NORM_EPS) * gain_ref[...] * gate_ref[rows, :].astype(F32)
    out_ref[rows, :] = y.astype(out_ref.dtype)


def _hgrn_chunk_stages(q_ref, k_ref, cum_ref, v_ref, rows, c2_ref, st_ref, o_scr, reverse, accumulate, lane_is):
    c = CHUNK
    q, k = q_ref[rows, :].astype(F32), k_ref[rows, :].astype(F32)
    cum, v = cum_ref[rows, :], v_ref[rows, :]
    c2_ref[...] = cum - jnp.log(k) * LOG2E
    q_in = (q * jnp.exp2(cum)).astype(BF16)
    last = cum[0:1] if reverse else cum[c - 1:c]
    kv_new = _dot_tn(v, (k * jnp.exp2(last - cum)).astype(BF16))
    a_off = _hgrn_offdiag(q, k, cum, reverse)
    summed = _hgrn_diag_sums(q, cum, c2_ref)
    yield
    a = a_off + _hgrn_diag_assemble(summed, reverse, lane_is)[:, 0:c]
    o_intra = _dot(a.astype(BF16), v)
    yield
    st = st_ref[...]
    o_inter = _dot_nt(q_in, st.astype(BF16))
    st_ref[...] = st * jnp.exp2(last) + kv_new
    yield
    if accumulate:
        o_scr[rows, :] += o_inter + o_intra
    else:
        o_scr[rows, :] = o_inter + o_intra


def _run_interleaved(stages):
    alive = list(stages)
    while alive:
        still = []
        for g in alive:
            try:
                next(g)
                still.append(g)
            except StopIteration:
                pass
        alive = still


def _lane_masks():
    lane = lax.broadcasted_iota(jnp.int32, (SUB, HEAD_DIM), 1)
    cache = {}

    def lane_is(i):
        if i not in cache:
            cache[i] = lane == i
        return cache[i]

    return lane_is


def _scan_loops(nc, step):
    groups = nc // UNROLL
    lax.fori_loop(0, groups // 2, lambda i, carry: (step(i * UNROLL, False), carry)[1], 0)
    lax.fori_loop(groups // 2, groups, lambda i, carry: (step(i * UNROLL, True), carry)[1], 0)


def _chunk_index(nc, first, u, reverse):
    return (nc - 1 - first - u) if reverse else first + u


def _chunk_rows(nc, first, u, reverse):
    start = _chunk_index(nc, first, u, reverse) * CHUNK
    return pl.ds(start if isinstance(start, int) else pl.multiple_of(start, CHUNK), CHUNK)


def _mlstm_chunk_stages(q_scr, k_scr, v_ref, gates_ref, ci, st_ref, m_ref, o_scr, reverse, accumulate):
    c = CHUNK
    rows = pl.ds(pl.multiple_of(ci * c, c), c)
    ig_row, cum_row = (1, 3) if reverse else (0, 2)
    q, k = q_scr[rows, :], k_scr[rows, :]
    v_ext = jnp.concatenate([v_ref[rows, :], jnp.ones((c, HEAD_DIM), BF16)], axis=1)
    g_rows = gates_ref[ci]
    hi = g_rows.astype(BF16).astype(F32)
    mid = (g_rows - hi).astype(BF16).astype(F32)
    lo = g_rows - hi - mid
    g3 = jnp.concatenate([hi, mid, lo, jnp.zeros_like(lo)], axis=0).astype(BF16)
    part = lax.broadcasted_iota(jnp.int32, (4 * GATE_ROWS, HEAD_DIM), 0)
    pick = lambda r: ((part % GATE_ROWS == r) & (part < 3 * GATE_ROWS)).astype(BF16)
    ig_c, cum_c = _dot_tn(g3, pick(ig_row)), _dot_tn(g3, pick(cum_row))
    qk = _dot_nt(q, k)
    yield
    ig_r, cum_r = g_rows[ig_row:ig_row + 1, :], g_rows[cum_row:cum_row + 1, :]
    t_idx = lax.broadcasted_iota(jnp.int32, (c, c), 0)
    s_idx = lax.broadcasted_iota(jnp.int32, (c, c), 1)
    causal = (s_idx >= t_idx) if reverse else (s_idx <= t_idx)
    log_d = jnp.where(causal, cum_c[:, 0:c] - cum_r + ig_r, -jnp.inf)
    row_max = jnp.max(log_d, axis=-1, keepdims=True)
    last = cum_c[0:1] if reverse else cum_c[c - 1:c]
    log_w = last - cum_c + ig_c
    w_max = jnp.max(log_w, axis=0, keepdims=True)
    yield
    m = m_ref[...]
    log_inter = cum_c + m
    m_t = jnp.maximum(row_max, log_inter)
    w_inter = jnp.exp2(log_inter - m_t)
    scores = qk * jnp.exp2(log_d - m_t[:, 0:c])
    intra = _dot(scores.astype(BF16), v_ext)
    m_new = jnp.maximum(last + m, w_max)
    m_ref[...] = m_new
    carry = jnp.exp2(last + m - m_new)
    kv_new = _dot_tn(v_ext, (k.astype(F32) * jnp.exp2(log_w - m_new)).astype(BF16))
    yield
    st = st_ref[...]
    inter = _dot_nt(q, st.astype(BF16))
    st_ref[...] = carry * st + kv_new
    yield
    num = intra[:, 0:HEAD_DIM] + w_inter * inter[:, 0:HEAD_DIM]
    den = intra[:, HEAD_DIM:2 * HEAD_DIM] + w_inter * inter[:, HEAD_DIM:2 * HEAD_DIM]
    h_out = num / jnp.maximum(jnp.abs(den), jnp.exp2(-m_t))
    if accumulate:
        o_scr[rows, :] += h_out
    else:
        o_scr[rows, :] = h_out


def _mlstm_conv(mq_ref, mk_ref, cwq_ref, cwk_ref, cbq_ref, cbk_ref, pad_scr, q_scr, k_scr, seq):
    c = CHUNK
    hd = HEAD_DIM
    halo = CONV_HALO
    pad_scr[0:halo, :] = jnp.zeros((halo, 2 * hd), BF16)
    pad_scr[halo + seq:2 * halo + seq, :] = jnp.zeros((halo, 2 * hd), BF16)
    pad_scr[halo:halo + seq, 0:hd] = mq_ref[...]
    pad_scr[halo:halo + seq, hd:2 * hd] = mk_ref[...]
    out_i = lax.broadcasted_iota(jnp.int32, (c, CONV_WIDTH * CONV_WIN), 0)
    col_i = lax.broadcasted_iota(jnp.int32, (c, CONV_WIDTH * CONV_WIN), 1)
    shifts = (col_i % CONV_WIN == out_i + col_i // CONV_WIN - CONV_WIDTH // 2 + halo).astype(BF16)
    taps = jnp.concatenate([cwq_ref[...], cwk_ref[...]], axis=1).astype(BF16)
    bias = jnp.concatenate([cbq_ref[...], cbk_ref[...]], axis=1)

    def conv_body(i, carry):
        start = pl.multiple_of(i * c, c)
        win = pad_scr[pl.ds(start, CONV_WIN), :]
        scaled = jnp.concatenate([win * taps[j:j + 1, :] for j in range(CONV_WIDTH)], axis=0)
        y = bias + _dot(shifts, scaled)
        y = y * _sigmoid(y)
        q_scr[pl.ds(start, c), :] = y[:, 0:hd].astype(BF16)
        k_scr[pl.ds(start, c), :] = (y[:, hd:2 * hd] * (hd ** -0.5)).astype(BF16)
        return carry

    lax.fori_loop(0, seq // c, conv_body, 0, unroll=8)


def _mixer_kernel(hq_ref, hkf_ref, hkb_ref, hcf_ref, hcb_ref, hv_ref, hg_ref, hgain_ref,
                  mq_ref, mk_ref, cwq_ref, cwk_ref, cbq_ref, cbk_ref, mv_ref, mo_ref, gates_ref, mgain_ref,
                  hout_ref, mout_ref,
                  ho_scr, c2_scr, hstf_scr, hstb_scr,
                  pad_scr, q_scr, k_scr, mo_scr, mstf_scr, mstb_scr, mf_scr, mb_scr, *, seq):
    nc = seq // CHUNK
    _mlstm_conv(mq_ref, mk_ref, cwq_ref, cwk_ref, cbq_ref, cbk_ref, pad_scr, q_scr, k_scr, seq)
    for ref in (hstf_scr, hstb_scr, mstf_scr, mstb_scr, mf_scr, mb_scr):
        ref[...] = jnp.zeros_like(ref)

    def finish_stages(first):
        stages = []
        for u in range(UNROLL):
            for reverse in (False, True):
                rows = _chunk_rows(nc, first, u, reverse)
                stages.append(_finish_rows(rows, ho_scr, hg_ref, hgain_ref, hout_ref))
                stages.append(_finish_rows(rows, mo_scr, mo_ref, mgain_ref, mout_ref))
        return stages

    def step(first, accumulate):
        lane_is = _lane_masks()
        stages = finish_stages(first - UNROLL) if accumulate else []
        for u in range(UNROLL):
            for reverse in (False, True):
                stages.append(_hgrn_chunk_stages(
                    hq_ref, hkb_ref if reverse else hkf_ref, hcb_ref if reverse else hcf_ref, hv_ref,
                    _chunk_rows(nc, first, u, reverse), c2_scr.at[2 * u + int(reverse)],
                    hstb_scr if reverse else hstf_scr, ho_scr, reverse, accumulate, lane_is))
                stages.append(_mlstm_chunk_stages(
                    q_scr, k_scr, mv_ref, gates_ref, _chunk_index(nc, first, u, reverse),
                    mstb_scr if reverse else mstf_scr, mb_scr if reverse else mf_scr,
                    mo_scr, reverse, accumulate))
        _run_interleaved(stages)

    _scan_loops(nc, step)
    _run_interleaved(finish_stages(nc - UNROLL))


def _mixer_scan(hq, hkf, hkb, hcf, hcb, hv, hg, hgain, mqk, conv_w, conv_b, mv, mo, gates, mgain, layer):
    b, s, _ = hq.shape
    hd = HEAD_DIM
    blk = lambda off=0: pl.BlockSpec((None, s, hd), lambda i, h: (i, 0, h + off))
    par = lambda rows, off=0: pl.BlockSpec((None, rows, hd), lambda i, h: (layer, 0, h + off))
    out = jax.ShapeDtypeStruct((b, s, GROUP_WIDTH), BF16)
    return pl.pallas_call(
        functools.partial(_mixer_kernel, seq=s),
        grid=(b, HEADS),
        in_specs=[blk(), blk(), blk(), blk(), blk(), blk(), blk(), par(1),
                  blk(), blk(HEADS), par(CONV_WIDTH), par(CONV_WIDTH, HEADS), par(1), par(1, HEADS),
                  blk(), blk(),
                  pl.BlockSpec((None, s // CHUNK, None, GATE_ROWS, CHUNK), lambda i, h: (i, 0, h, 0, 0)),
                  par(1)],
        out_specs=[blk(), blk()],
        out_shape=[out, out],
        scratch_shapes=[pltpu.VMEM((s, hd), F32), pltpu.VMEM((2 * UNROLL, CHUNK, hd), F32),
                        pltpu.VMEM((hd, hd), F32), pltpu.VMEM((hd, hd), F32),
                        pltpu.VMEM((s + 2 * CONV_HALO, 2 * hd), BF16),
                        pltpu.VMEM((s, hd), BF16), pltpu.VMEM((s, hd), BF16), pltpu.VMEM((s, hd), F32),
                        pltpu.VMEM((2 * hd, hd), F32), pltpu.VMEM((2 * hd, hd), F32),
                        pltpu.VMEM((1, hd), F32), pltpu.VMEM((1, hd), F32)],
        compiler_params=_params(("arbitrary", "arbitrary")),
        name="mixer_scan",
    )(hq, hkf, hkb, hcf, hcb, hv, hg, hgain, mqk, mqk, conv_w, conv_w, conv_b, conv_b, mv, mo, gates, mgain)


def _attn_kernel(x_ref, hmix_ref, mmix_ref, wout_ref, g_ref, wq_ref, kv_ref, wo_ref, out_ref):
    w = GROUP_WIDTH
    hd = XATTN_HEAD_DIM
    x1 = x_ref[...] + _dot(hmix_ref[...], wout_ref[0:w, :]) + _dot(mmix_ref[...], wout_ref[w:2 * w, :])
    xn = _rms_scale(x1, g_ref[...]).astype(BF16)
    q = (_dot(xn, wq_ref[...]) * (hd ** -0.5)).astype(BF16)
    heads = []
    for h in range(HEADS):
        kh = kv_ref[:, h * hd:(h + 1) * hd]
        vh = kv_ref[:, D_MODEL + h * hd:D_MODEL + (h + 1) * hd]
        sc = _dot_nt(q[:, h * hd:(h + 1) * hd], kh)
        p = jnp.exp(sc - jnp.max(sc, axis=-1, keepdims=True))
        o = _dot(p.astype(BF16), vh) / jnp.sum(p, axis=-1, keepdims=True)
        heads.append(o.astype(BF16))
    out_ref[...] = x1 + _dot(jnp.concatenate(heads, axis=1), wo_ref[...])


def _attn_block(x, hmix, mmix, w_out, norm_xattn, w_xq, kv, w_xo, layer):
    b, s, d = x.shape
    tm = min(ROW_TILE, s)
    m = kv.shape[2]
    row = lambda i, j: (i, j, 0)
    fixed3 = lambda i, j: (layer, 0, 0)
    wspec = lambda: pl.BlockSpec((None, d, d), fixed3, pipeline_mode=pl.Buffered(1))
    return pl.pallas_call(
        _attn_kernel,
        grid=(b, s // tm),
        in_specs=[
            pl.BlockSpec((None, tm, d), row),
            pl.BlockSpec((None, tm, GROUP_WIDTH), row),
            pl.BlockSpec((None, tm, GROUP_WIDTH), row),
            wspec(),
            pl.BlockSpec((None, 1, d), fixed3),
            wspec(),
            pl.BlockSpec((None, None, m, 2 * d), lambda i, j: (layer, i, 0, 0)),
            wspec(),
        ],
        out_specs=pl.BlockSpec((None, tm, d), row),
        out_shape=jax.ShapeDtypeStruct((b, s, d), F32),
        compiler_params=_params(("arbitrary", "arbitrary")),
        name="attn_block",
    )(x, hmix, mmix, w_out, norm_xattn, w_xq, kv, w_xo)


def _mlp_kernel(x_ref, g_ref, wup_ref, wdn_ref, gfin_ref, out_ref, *, final):
    x = x_ref[...]
    xn = _rms_scale(x, g_ref[...]).astype(BF16)
    acc = x
    step = D_MODEL
    for c0 in range(0, D_FF, step):
        hcol = jnp.maximum(_dot(xn, wup_ref[:, c0:c0 + step]), 0.0)
        acc = acc + _dot((hcol * hcol).astype(BF16), wdn_ref[c0:c0 + step, :])
    if final:
        acc = _rms_scale(acc, gfin_ref[...])
    out_ref[...] = acc


def _mlp_block(x, norm_mlp, w_up, w_down, norm_final, layer, final):
    b, s, d = x.shape
    tm = min(ROW_TILE, s)
    row = lambda i, j: (i, j, 0)
    fixed3 = lambda i, j: (layer, 0, 0)
    return pl.pallas_call(
        functools.partial(_mlp_kernel, final=final),
        grid=(b, s // tm),
        in_specs=[
            pl.BlockSpec((None, tm, d), row),
            pl.BlockSpec((None, 1, d), fixed3),
            pl.BlockSpec((None, d, D_FF), fixed3, pipeline_mode=pl.Buffered(1)),
            pl.BlockSpec((None, D_FF, d), fixed3, pipeline_mode=pl.Buffered(1)),
            pl.BlockSpec((1, d), lambda i, j: (0, 0)),
        ],
        out_specs=pl.BlockSpec((None, tm, d), row),
        out_shape=jax.ShapeDtypeStruct((b, s, d), F32),
        compiler_params=_params(("arbitrary", "arbitrary")),
        name="mlp_block",
    )(x, norm_mlp, w_up, w_down, norm_final)


def kernel(x, mem, norm_mix, norm_xattn, norm_mem, norm_mlp, norm_final, w_in, b_in, hgrn_lb_logits, hgrn_norm,
           mlstm_conv_w, mlstm_conv_b, mlstm_norm, w_out, w_xq, w_xkv, w_xo, w_up, w_down):
    depth = w_in.shape[0]
    b, s, d = x.shape
    assert d == D_MODEL and s % (2 * UNROLL * CHUNK) == 0 and s % min(ROW_TILE, s) == 0
    assert w_in.shape[-1] == MAIN_COLS + 4 * HEADS

    row3 = lambda t: t.reshape(depth, 1, t.shape[-1])
    w_gate = w_in[:, :, MAIN_COLS:].reshape(depth, d, 4, HEADS).transpose(0, 3, 2, 1)
    w_gate = jnp.pad(w_gate, ((0, 0), (0, 0), (0, GATE_ROWS - 4), (0, 0))).reshape(depth, HEADS * GATE_ROWS, d)
    b_gate = b_in[:, MAIN_COLS:].reshape(depth, 4, HEADS).transpose(0, 2, 1)
    b_gate = jnp.pad(b_gate, ((0, 0), (0, 0), (0, GATE_ROWS - 4))).reshape(depth, HEADS * GATE_ROWS, 1)
    w_main = w_in.astype(BF16)
    b_main = row3(b_in)
    w_gate = w_gate.astype(BF16)
    w_out_b, w_xq_b, w_xkv_b, w_xo_b = (t.astype(BF16) for t in (w_out, w_xq, w_xkv, w_xo))
    w_up_b, w_down_b = w_up.astype(BF16), w_down.astype(BF16)
    norm_mix3, norm_xattn3, norm_mlp3 = row3(norm_mix), row3(norm_xattn), row3(norm_mlp)
    hgrn_norm3, mlstm_norm3, conv_b3 = row3(hgrn_norm), row3(mlstm_norm), row3(mlstm_conv_b)
    norm_final2 = norm_final.reshape(1, d)

    lb = _lower_bounds(hgrn_lb_logits)
    kv = _mem_kv(mem, norm_mem, w_xkv_b)

    for l in range(depth):
        (hq, hkf, hkb, hcf, hcb, hv, hg, mqk, mv, mo, gates) = _in_proj(
            x, l, norm_mix3, w_main, b_main, w_gate, b_gate, lb)
        hmix, mmix = _mixer_scan(hq, hkf, hkb, hcf, hcb, hv, hg, hgrn_norm3,
                                 mqk, mlstm_conv_w, conv_b3, mv, mo, gates, mlstm_norm3, l)
        x = _attn_block(x, hmix, mmix, w_out_b, norm_xattn3, w_xq_b, kv, w_xo_b, l)
        x = _mlp_block(x, norm_mlp3, w_up_b, w_down_b, norm_final2, l, final=(l == depth - 1))
    return x
```

```python
import functools

import jax
import jax.numpy as jnp
from jax import lax
from jax.experimental import pallas as pl
from jax.experimental.pallas import tpu as pltpu

F32 = jnp.float32
BF16 = jnp.bfloat16

D_MODEL = 1024
HEADS = 4
HEAD_DIM = 128
GROUP_WIDTH = HEADS * HEAD_DIM
CONV_WIDTH = 5
XATTN_HEAD_DIM = D_MODEL // HEADS
D_FF = 4 * D_MODEL
CHUNK = 64
SUB = 8
GATE_ROWS = 8
UNROLL = 8
STAGE_SKEW = 1
NORM_EPS = 1e-6
LOG2E = 1.4426950408889634
MAIN_COLS = 9 * GROUP_WIDTH
CONV_HALO = 16
CONV_WIN = CHUNK + 2 * CONV_HALO
ROW_TILE = 512
VMEM_LIMIT = 56 * 1024 * 1024


def _dot(a, b):
    return jnp.dot(a, b, preferred_element_type=F32)


def _dot_nt(a, b):
    return lax.dot_general(a, b, (((1,), (1,)), ((), ())), preferred_element_type=F32)


def _dot_tn(a, b):
    return lax.dot_general(a, b, (((0,), (0,)), ((), ())), preferred_element_type=F32)


def _sigmoid(z):
    return 1.0 / (1.0 + jnp.exp(-z))


def _log_sigmoid(z):
    return jnp.minimum(z, 0.0) - jnp.log(1.0 + jnp.exp(-jnp.abs(z)))


def _rms_scale(x, gain):
    return x * lax.rsqrt(jnp.mean(x * x, axis=-1, keepdims=True) + NORM_EPS) * gain


def _split3_rows(x):
    hi = x.astype(BF16)
    r1 = x - hi.astype(F32)
    mid = r1.astype(BF16)
    lo = (r1 - mid.astype(F32)).astype(BF16)
    return jnp.concatenate([hi, mid, lo], axis=0)


def _tri3(reverse):
    r = lax.broadcasted_iota(jnp.int32, (CHUNK, 3 * CHUNK), 0)
    s = lax.broadcasted_iota(jnp.int32, (CHUNK, 3 * CHUNK), 1) % CHUNK
    return ((s >= r) if reverse else (s <= r)).astype(BF16)


def _params(semantics):
    return pltpu.CompilerParams(dimension_semantics=semantics, vmem_limit_bytes=VMEM_LIMIT)


def _lb_kernel(logit_ref, lb_ref, *, depth):
    for d in range(2):
        rows = [logit_ref[d * depth + l:d * depth + l + 1, :] for l in range(depth)]
        mx = functools.reduce(jnp.maximum, rows)
        es = [jnp.exp(r - mx) for r in rows]
        tot = functools.reduce(lambda a, b: a + b, es)
        c = None
        c0 = None
        for l in range(depth):
            p = es[l] / tot
            c = p if c is None else c + p
            if l == 0:
                c0 = c
            lb_ref[2 * l + d:2 * l + d + 1, :] = c - c0


def _lower_bounds(logits):
    _, depth, width = logits.shape
    out = pl.pallas_call(
        functools.partial(_lb_kernel, depth=depth),
        out_shape=jax.ShapeDtypeStruct((2 * depth, width), F32),
        name="lower_bounds",
    )(logits.astype(F32).reshape(2 * depth, width))
    return out.reshape(depth, 2, width)


def _mem_kv_kernel(mem_ref, g_ref, w_ref, kv_ref):
    memn = _rms_scale(mem_ref[...], g_ref[...]).astype(BF16)
    kv_ref[...] = _dot(memn, w_ref[...]).astype(BF16)


def _mem_kv(mem, norm_mem, w_xkv):
    b, m, d = mem.shape
    depth = norm_mem.shape[0]
    return pl.pallas_call(
        _mem_kv_kernel,
        grid=(depth, b),
        in_specs=[
            pl.BlockSpec((None, m, d), lambda l, i: (i, 0, 0)),
            pl.BlockSpec((None, 1, d), lambda l, i: (l, 0, 0)),
            pl.BlockSpec((None, d, 2 * d), lambda l, i: (l, 0, 0)),
        ],
        out_specs=pl.BlockSpec((None, None, m, 2 * d), lambda l, i: (l, i, 0, 0)),
        out_shape=jax.ShapeDtypeStruct((depth, b, m, 2 * d), BF16),
        compiler_params=_params(("arbitrary", "arbitrary")),
        name="mem_kv",
    )(mem, norm_mem.reshape(depth, 1, d), w_xkv)


def _in_proj_kernel(x_ref, g_ref, w_ref, b_ref, wg_ref, bg_ref, lb_ref,
                    hq_ref, hkf_ref, hkb_ref, hcf_ref, hcb_ref, hv_ref, hg_ref,
                    mqk_ref, mv_ref, mo_ref, gate_ref):
    w = GROUP_WIDTH
    c = CHUNK
    n_chunks = x_ref.shape[0] // c
    xn = _rms_scale(x_ref[...], g_ref[...]).astype(BF16)
    tri_f, tri_b = _tri3(False), _tri3(True)

    def proj(group):
        c0 = group * w
        return _dot(xn, w_ref[:, c0:c0 + w]) + b_ref[:, c0:c0 + w]

    def silu_to(z, ref):
        ref[...] = (z * _sigmoid(z)).astype(BF16)

    def forget_rows(z, lb, tri, k_ref, cum_ref, r0, r1):
        z = z[r0:r1]
        e = jnp.exp(-jnp.abs(z))
        log_sig = jnp.minimum(z, 0.0) - jnp.log(1.0 + e)
        a = jnp.log(lb)
        t = jnp.log1p(-lb) + log_sig
        lf2 = (jnp.maximum(a, t) + jnp.log(1.0 + jnp.exp(-jnp.abs(a - t)))) * LOG2E
        k_ref[r0:r1, :] = ((1.0 - lb) * (jnp.where(z >= 0, e, 1.0) / (1.0 + e))).astype(BF16)
        for j in range((r1 - r0) // c):
            cum_ref[r0 + j * c:r0 + (j + 1) * c, :] = _dot(tri, _split3_rows(lf2[j * c:(j + 1) * c]))

    half = x_ref.shape[0] // 2
    lb_f, lb_b = lb_ref[0:1, :], lb_ref[1:2, :]
    z_q, z_ff = proj(0), proj(1)
    z_fb = proj(2)
    silu_to(z_q, hq_ref)
    z_v = proj(3)
    forget_rows(z_ff, lb_f, tri_f, hkf_ref, hcf_ref, 0, half)
    z_g = proj(4)
    forget_rows(z_ff, lb_f, tri_f, hkf_ref, hcf_ref, half, 2 * half)
    hv_ref[...] = z_v.astype(BF16)
    z_mq = proj(5)
    forget_rows(z_fb, lb_b, tri_b, hkb_ref, hcb_ref, 0, half)
    z_mk = proj(6)
    forget_rows(z_fb, lb_b, tri_b, hkb_ref, hcb_ref, half, 2 * half)
    mqk_ref[:, 0:w] = z_mq.astype(BF16)
    z_mv = proj(7)
    silu_to(z_g, hg_ref)
    mqk_ref[:, w:2 * w] = z_mk.astype(BF16)
    z_mo = proj(8)
    mv_ref[...] = z_mv.astype(BF16)
    mo_ref[...] = _sigmoid(z_mo).astype(BF16)
    tm = x_ref.shape[0]
    zg = _dot_nt(wg_ref[...], xn) + bg_ref[...]
    kind = lax.broadcasted_iota(jnp.int32, zg.shape, 0) % GATE_ROWS
    g2 = jnp.where(kind >= 2, _log_sigmoid(zg), zg) * LOG2E
    g3 = _split3_rows(g2)
    u = lax.broadcasted_iota(jnp.int32, (tm, tm), 0)
    s = lax.broadcasted_iota(jnp.int32, (tm, tm), 1)
    same_chunk = (u // c) == (s // c)

    def lane_cumsum(tri):
        parts = _dot(g3, tri.astype(BF16))
        n = zg.shape[0]
        return parts[0:n] + parts[n:2 * n] + parts[2 * n:3 * n]

    prefix = lane_cumsum(same_chunk & (u <= s))
    suffix = lane_cumsum(same_chunk & (u >= s))
    rows = jnp.where(kind == 2, prefix, jnp.where(kind == 3, suffix, g2))
    for j in range(n_chunks):
        for h in range(HEADS):
            gate_ref[j, h] = rows[h * GATE_ROWS:(h + 1) * GATE_ROWS, j * c:(j + 1) * c]


def _in_proj(x, layer, norm_mix, w_main, b_main, w_gate, b_gate, lb):
    b, s, d = x.shape
    tm = min(ROW_TILE, s)
    w = GROUP_WIDTH
    grid = (b, s // tm)
    row = lambda i, j: (i, j, 0)
    fixed3 = lambda i, j: (layer, 0, 0)
    act = lambda width, dtype: jax.ShapeDtypeStruct((b, s, width), dtype)
    out_shapes = [act(w, BF16), act(w, BF16), act(w, BF16), act(w, F32), act(w, F32), act(w, BF16),
                  act(w, BF16), act(2 * w, BF16), act(w, BF16), act(w, BF16)]
    out_specs = [pl.BlockSpec((None, tm, o.shape[-1]), row) for o in out_shapes]
    out_shapes.append(jax.ShapeDtypeStruct((b, s // CHUNK, HEADS, GATE_ROWS, CHUNK), F32))
    out_specs.append(pl.BlockSpec((None, tm // CHUNK, HEADS, GATE_ROWS, CHUNK), lambda i, j: (i, j, 0, 0, 0)))
    return pl.pallas_call(
        _in_proj_kernel,
        grid=grid,
        in_specs=[
            pl.BlockSpec((None, tm, d), row),
            pl.BlockSpec((None, 1, d), fixed3),
            pl.BlockSpec((None, d, MAIN_COLS), fixed3, pipeline_mode=pl.Buffered(1)),
            pl.BlockSpec((None, 1, MAIN_COLS), fixed3),
            pl.BlockSpec((None, HEADS * GATE_ROWS, d), fixed3),
            pl.BlockSpec((None, HEADS * GATE_ROWS, 1), fixed3),
            pl.BlockSpec((None, 2, w), fixed3),
        ],
        out_specs=out_specs,
        out_shape=out_shapes,
        compiler_params=_params(("arbitrary", "arbitrary")),
        name="in_proj",
    )(x, norm_mix, w_main, b_main, w_gate, b_gate, lb)


def _hgrn_offdiag(q, k, cum, reverse):
    c = CHUNK
    r64 = lax.broadcasted_iota(jnp.int32, (c, c), 0)
    c64 = lax.broadcasted_iota(jnp.int32, (c, c), 1)
    a = None
    width = c // 2
    while width >= SUB:
        span = 2 * width
        expo = []
        for base in range(0, c, span):
            r = base + width if reverse else base + width - 1
            ref, lo, hi = cum[r:r + 1], cum[base:base + width], cum[base + width:base + span]
            expo += [lo - ref, ref - hi] if reverse else [ref - lo, hi - ref]
        e = jnp.exp2(jnp.concatenate(expo, axis=0))
        qe, ke = q * e, k * e
        zeros = jnp.zeros((width, HEAD_DIM), F32)
        qparts, kparts = [], []
        for base in range(0, c, span):
            lo, hi = slice(base, base + width), slice(base + width, base + span)
            if reverse:
                qparts += [qe[lo], zeros]
                kparts += [zeros, ke[hi]]
            else:
                qparts += [zeros, qe[hi]]
                kparts += [ke[lo], zeros]
        a_l = _dot_nt(jnp.concatenate(qparts, axis=0).astype(BF16), jnp.concatenate(kparts, axis=0).astype(BF16))
        if span < c:
            a_l = jnp.where((r64 ^ c64) < span, a_l, 0.0)
        a = a_l if a is None else a + a_l
        width //= 2
    return a


def _hgrn_diag(q, cum, c2_ref, sel_ref, mask):
    rows = []
    for b0 in range(0, CHUNK, SUB):
        cb, qb = cum[b0:b0 + SUB], q[b0:b0 + SUB]
        tiles = []
        for s in range(SUB):
            c2s = c2_ref[b0 + s:b0 + s + 1, :]
            tiles.append(qb * jnp.exp2(jnp.minimum(cb - c2s, 0.0)))
        rows.append(jnp.concatenate(tiles, axis=1))
    return _dot(jnp.concatenate(rows, axis=0).astype(BF16), sel_ref[...]) * mask


def _hgrn_diag_tables(sel_ref, mask_ref):
    k = lax.broadcasted_iota(jnp.int32, sel_ref.shape, 0)
    lane = lax.broadcasted_iota(jnp.int32, sel_ref.shape, 1)
    sel_ref[...] = (k // HEAD_DIM == lane % SUB).astype(BF16)
    t = lax.broadcasted_iota(jnp.int32, (CHUNK, HEAD_DIM), 0)
    s = lax.broadcasted_iota(jnp.int32, (CHUNK, HEAD_DIM), 1)
    same_block = (t // SUB) == (s // SUB)
    mask_ref[0] = (same_block & (s <= t)).astype(F32)
    mask_ref[1] = (same_block & (s >= t)).astype(F32)


def _finish_rows(rows, o_scr, gate_ref, gain_ref, out_ref):
    total = o_scr[rows, :]
    sq = _dot((total * total).astype(BF16), jnp.ones((HEAD_DIM, HEAD_DIM), BF16))
    yield
    y = total * lax.rsqrt(sq * (1.0 / HEAD_DIM) + NORM_EPS) * gain_ref[...] * gate_ref[rows, :].astype(F32)
    out_ref[rows, :] = y.astype(out_ref.dtype)


def _hgrn_chunk_stages(q_ref, k_ref, cum_ref, v_ref, rows, c2_ref, st_ref, o_scr, reverse, accumulate,
                       sel_ref, mask_ref):
    c = CHUNK
    q, k = q_ref[rows, :].astype(F32), k_ref[rows, :].astype(F32)
    cum, v = cum_ref[rows, :], v_ref[rows, :]
    c2_ref[...] = cum - jnp.log(k) * LOG2E
    q_in = (q * jnp.exp2(cum)).astype(BF16)
    last = cum[0:1] if reverse else cum[c - 1:c]
    kv_new = _dot_tn(v, (k * jnp.exp2(last - cum)).astype(BF16))
    a_off = _hgrn_offdiag(q, k, cum, reverse)
    diag = _hgrn_diag(q, cum, c2_ref, sel_ref, mask_ref[int(reverse)])
    yield
    o_intra = _dot((a_off + diag[:, 0:c]).astype(BF16), v)
    yield
    st = st_ref[...]
    o_inter = _dot_nt(q_in, st.astype(BF16))
    st_ref[...] = st * jnp.exp2(last) + kv_new
    yield
    if accumulate:
        o_scr[rows, :] += o_inter + o_intra
    else:
        o_scr[rows, :] = o_inter + o_intra


def _delayed(rounds, stages):
    for _ in range(rounds):
        yield
    yield from stages


def _run_interleaved(stages):
    alive = list(stages)
    while alive:
        still = []
        for g in alive:
            try:
                next(g)
                still.append(g)
            except StopIteration:
                pass
        alive = still


def _scan_loops(nc, step):
    groups = nc // UNROLL
    lax.fori_loop(0, groups // 2, lambda i, carry: (step(i * UNROLL, False), carry)[1], 0)
    lax.fori_loop(groups // 2, groups, lambda i, carry: (step(i * UNROLL, True), carry)[1], 0)


def _chunk_index(nc, first, u, reverse):
    return (nc - 1 - first - u) if reverse else first + u


def _chunk_rows(nc, first, u, reverse):
    start = _chunk_index(nc, first, u, reverse) * CHUNK
    return pl.ds(start if isinstance(start, int) else pl.multiple_of(start, CHUNK), CHUNK)


def _mlstm_chunk_stages(q_scr, k_scr, v_ref, gates_ref, ci, st_ref, m_ref, o_scr, reverse, accumulate):
    c = CHUNK
    rows = pl.ds(pl.multiple_of(ci * c, c), c)
    ig_row, cum_row = (1, 3) if reverse else (0, 2)
    q, k = q_scr[rows, :], k_scr[rows, :]
    v_ext = jnp.concatenate([v_ref[rows, :], jnp.ones((c, HEAD_DIM), BF16)], axis=1)
    g_rows = gates_ref[ci]
    hi = g_rows.astype(BF16).astype(F32)
    mid = (g_rows - hi).astype(BF16).astype(F32)
    lo = g_rows - hi - mid
    g3 = jnp.concatenate([hi, mid, lo, jnp.zeros_like(lo)], axis=0).astype(BF16)
    part = lax.broadcasted_iota(jnp.int32, (4 * GATE_ROWS, HEAD_DIM), 0)
    pick = lambda r: ((part % GATE_ROWS == r) & (part < 3 * GATE_ROWS)).astype(BF16)
    ig_c, cum_c = _dot_tn(g3, pick(ig_row)), _dot_tn(g3, pick(cum_row))
    qk = _dot_nt(q, k)
    yield
    ig_r, cum_r = g_rows[ig_row:ig_row + 1, :], g_rows[cum_row:cum_row + 1, :]
    t_idx = lax.broadcasted_iota(jnp.int32, (c, c), 0)
    s_idx = lax.broadcasted_iota(jnp.int32, (c, c), 1)
    causal = (s_idx >= t_idx) if reverse else (s_idx <= t_idx)
    log_d = jnp.where(causal, cum_c[:, 0:c] - cum_r + ig_r, -jnp.inf)
    row_max = jnp.max(log_d, axis=-1, keepdims=True)
    last = cum_c[0:1] if reverse else cum_c[c - 1:c]
    log_w = last - cum_c + ig_c
    w_max = jnp.max(log_w, axis=0, keepdims=True)
    yield
    m = m_ref[...]
    log_inter = cum_c + m
    m_t = jnp.maximum(row_max, log_inter)
    w_inter = jnp.exp2(log_inter - m_t)
    scores = qk * jnp.exp2(log_d - m_t[:, 0:c])
    intra = _dot(scores.astype(BF16), v_ext)
    m_new = jnp.maximum(last + m, w_max)
    m_ref[...] = m_new
    carry = jnp.exp2(last + m - m_new)
    kv_new = _dot_tn(v_ext, (k.astype(F32) * jnp.exp2(log_w - m_new)).astype(BF16))
    yield
    st = st_ref[...]
    inter = _dot_nt(q, st.astype(BF16))
    st_ref[...] = carry * st + kv_new
    yield
    num = intra[:, 0:HEAD_DIM] + w_inter * inter[:, 0:HEAD_DIM]
    den = intra[:, HEAD_DIM:2 * HEAD_DIM] + w_inter * inter[:, HEAD_DIM:2 * HEAD_DIM]
    h_out = num / jnp.maximum(jnp.abs(den), jnp.exp2(-m_t))
    if accumulate:
        o_scr[rows, :] += h_out
    else:
        o_scr[rows, :] = h_out


def _mlstm_conv(mq_ref, mk_ref, cwq_ref, cwk_ref, cbq_ref, cbk_ref, pad_scr, q_scr, k_scr, seq):
    c = CHUNK
    hd = HEAD_DIM
    halo = CONV_HALO
    pad_scr[0:halo, :] = jnp.zeros((halo, 2 * hd), BF16)
    pad_scr[halo + seq:2 * halo + seq, :] = jnp.zeros((halo, 2 * hd), BF16)
    pad_scr[halo:halo + seq, 0:hd] = mq_ref[...]
    pad_scr[halo:halo + seq, hd:2 * hd] = mk_ref[...]
    out_i = lax.broadcasted_iota(jnp.int32, (c, CONV_WIDTH * CONV_WIN), 0)
    col_i = lax.broadcasted_iota(jnp.int32, (c, CONV_WIDTH * CONV_WIN), 1)
    shifts = (col_i % CONV_WIN == out_i + col_i // CONV_WIN - CONV_WIDTH // 2 + halo).astype(BF16)
    taps = jnp.concatenate([cwq_ref[...], cwk_ref[...]], axis=1).astype(BF16)
    bias = jnp.concatenate([cbq_ref[...], cbk_ref[...]], axis=1)

    def conv_body(i, carry):
        start = pl.multiple_of(i * c, c)
        win = pad_scr[pl.ds(start, CONV_WIN), :]
        scaled = jnp.concatenate([win * taps[j:j + 1, :] for j in range(CONV_WIDTH)], axis=0)
        y = bias + _dot(shifts, scaled)
        y = y * _sigmoid(y)
        q_scr[pl.ds(start, c), :] = y[:, 0:hd].astype(BF16)
        k_scr[pl.ds(start, c), :] = (y[:, hd:2 * hd] * (hd ** -0.5)).astype(BF16)
        return carry

    lax.fori_loop(0, seq // c, conv_body, 0, unroll=8)


def _mixer_kernel(hq_ref, hkf_ref, hkb_ref, hcf_ref, hcb_ref, hv_ref, hg_ref, hgain_ref,
                  mq_ref, mk_ref, cwq_ref, cwk_ref, cbq_ref, cbk_ref, mv_ref, mo_ref, gates_ref, mgain_ref,
                  hout_ref, mout_ref,
                  ho_scr, c2_scr, hstf_scr, hstb_scr, sel_scr, mask_scr,
                  pad_scr, q_scr, k_scr, mo_scr, mstf_scr, mstb_scr, mf_scr, mb_scr, *, seq):
    nc = seq // CHUNK
    _mlstm_conv(mq_ref, mk_ref, cwq_ref, cwk_ref, cbq_ref, cbk_ref, pad_scr, q_scr, k_scr, seq)
    for ref in (hstf_scr, hstb_scr, mstf_scr, mstb_scr, mf_scr, mb_scr):
        ref[...] = jnp.zeros_like(ref)
    _hgrn_diag_tables(sel_scr, mask_scr)

    def finish_stages(first):
        stages = []
        for u in range(UNROLL):
            for reverse in (False, True):
                rows = _chunk_rows(nc, first, u, reverse)
                stages.append(_finish_rows(rows, ho_scr, hg_ref, hgain_ref, hout_ref))
                stages.append(_finish_rows(rows, mo_scr, mo_ref, mgain_ref, mout_ref))
        return stages

    def step(first, accumulate):
        stages = finish_stages(first - UNROLL) if accumulate else []
        for u in range(UNROLL):
            for reverse in (False, True):
                stages.append(_delayed(u * STAGE_SKEW, _hgrn_chunk_stages(
                    hq_ref, hkb_ref if reverse else hkf_ref, hcb_ref if reverse else hcf_ref, hv_ref,
                    _chunk_rows(nc, first, u, reverse), c2_scr.at[2 * u + int(reverse)],
                    hstb_scr if reverse else hstf_scr, ho_scr, reverse, accumulate, sel_scr, mask_scr)))
                stages.append(_delayed(u * STAGE_SKEW, _mlstm_chunk_stages(
                    q_scr, k_scr, mv_ref, gates_ref, _chunk_index(nc, first, u, reverse),
                    mstb_scr if reverse else mstf_scr, mb_scr if reverse else mf_scr,
                    mo_scr, reverse, accumulate)))
        _run_interleaved(stages)

    _scan_loops(nc, step)
    _run_interleaved(finish_stages(nc - UNROLL))


def _mixer_scan(hq, hkf, hkb, hcf, hcb, hv, hg, hgain, mqk, conv_w, conv_b, mv, mo, gates, mgain, layer):
    b, s, _ = hq.shape
    hd = HEAD_DIM
    blk = lambda off=0: pl.BlockSpec((None, s, hd), lambda i, h: (i, 0, h + off))
    par = lambda rows, off=0: pl.BlockSpec((None, rows, hd), lambda i, h: (layer, 0, h + off))
    out = jax.ShapeDtypeStruct((b, s, GROUP_WIDTH), BF16)
    return pl.pallas_call(
        functools.partial(_mixer_kernel, seq=s),
        grid=(b, HEADS),
        in_specs=[blk(), blk(), blk(), blk(), blk(), blk(), blk(), par(1),
                  blk(), blk(HEADS), par(CONV_WIDTH), par(CONV_WIDTH, HEADS), par(1), par(1, HEADS),
                  blk(), blk(),
                  pl.BlockSpec((None, s // CHUNK, None, GATE_ROWS, CHUNK), lambda i, h: (i, 0, h, 0, 0)),
                  par(1)],
        out_specs=[blk(), blk()],
        out_shape=[out, out],
        scratch_shapes=[pltpu.VMEM((s, hd), F32), pltpu.VMEM((2 * UNROLL, CHUNK, hd), F32),
                        pltpu.VMEM((hd, hd), F32), pltpu.VMEM((hd, hd), F32),
                        pltpu.VMEM((SUB * hd, hd), BF16), pltpu.VMEM((2, CHUNK, hd), F32),
                        pltpu.VMEM((s + 2 * CONV_HALO, 2 * hd), BF16),
                        pltpu.VMEM((s, hd), BF16), pltpu.VMEM((s, hd), BF16), pltpu.VMEM((s, hd), F32),
                        pltpu.VMEM((2 * hd, hd), F32), pltpu.VMEM((2 * hd, hd), F32),
                        pltpu.VMEM((1, hd), F32), pltpu.VMEM((1, hd), F32)],
        compiler_params=_params(("arbitrary", "arbitrary")),
        name="mixer_scan",
    )(hq, hkf, hkb, hcf, hcb, hv, hg, hgain, mqk, mqk, conv_w, conv_w, conv_b, conv_b, mv, mo, gates, mgain)


def _attn_kernel(x_ref, hmix_ref, mmix_ref, wout_ref, g_ref, wq_ref, kv_ref, wo_ref, out_ref):
    w = GROUP_WIDTH
    hd = XATTN_HEAD_DIM
    x1 = x_ref[...] + _dot(hmix_ref[...], wout_ref[0:w, :]) + _dot(mmix_ref[...], wout_ref[w:2 * w, :])
    xn = _rms_scale(x1, g_ref[...]).astype(BF16)
    q = (_dot(xn, wq_ref[...]) * (hd ** -0.5)).astype(BF16)
    heads = []
    for h in range(HEADS):
        kh = kv_ref[:, h * hd:(h + 1) * hd]
        vh = kv_ref[:, D_MODEL + h * hd:D_MODEL + (h + 1) * hd]
        sc = _dot_nt(q[:, h * hd:(h + 1) * hd], kh)
        p = jnp.exp(sc - jnp.max(sc, axis=-1, keepdims=True))
        o = _dot(p.astype(BF16), vh) / jnp.sum(p, axis=-1, keepdims=True)
        heads.append(o.astype(BF16))
    out_ref[...] = x1 + _dot(jnp.concatenate(heads, axis=1), wo_ref[...])


def _attn_block(x, hmix, mmix, w_out, norm_xattn, w_xq, kv, w_xo, layer):
    b, s, d = x.shape
    tm = min(ROW_TILE, s)
    m = kv.shape[2]
    row = lambda i, j: (i, j, 0)
    fixed3 = lambda i, j: (layer, 0, 0)
    wspec = lambda: pl.BlockSpec((None, d, d), fixed3, pipeline_mode=pl.Buffered(1))
    return pl.pallas_call(
        _attn_kernel,
        grid=(b, s // tm),
        in_specs=[
            pl.BlockSpec((None, tm, d), row),
            pl.BlockSpec((None, tm, GROUP_WIDTH), row),
            pl.BlockSpec((None, tm, GROUP_WIDTH), row),
            wspec(),
            pl.BlockSpec((None, 1, d), fixed3),
            wspec(),
            pl.BlockSpec((None, None, m, 2 * d), lambda i, j: (layer, i, 0, 0)),
            wspec(),
        ],
        out_specs=pl.BlockSpec((None, tm, d), row),
        out_shape=jax.ShapeDtypeStruct((b, s, d), F32),
        compiler_params=_params(("arbitrary", "arbitrary")),
        name="attn_block",
    )(x, hmix, mmix, w_out, norm_xattn, w_xq, kv, w_xo)


def _mlp_kernel(x_ref, g_ref, wup_ref, wdn_ref, gfin_ref, out_ref, *, final):
    x = x_ref[...]
    xn = _rms_scale(x, g_ref[...]).astype(BF16)
    acc = x
    step = D_MODEL
    for c0 in range(0, D_FF, step):
        hcol = jnp.maximum(_dot(xn, wup_ref[:, c0:c0 + step]), 0.0)
        acc = acc + _dot((hcol * hcol).astype(BF16), wdn_ref[c0:c0 + step, :])
    if final:
        acc = _rms_scale(acc, gfin_ref[...])
    out_ref[...] = acc


def _mlp_block(x, norm_mlp, w_up, w_down, norm_final, layer, final):
    b, s, d = x.shape
    tm = min(ROW_TILE, s)
    row = lambda i, j: (i, j, 0)
    fixed3 = lambda i, j: (layer, 0, 0)
    return pl.pallas_call(
        functools.partial(_mlp_kernel, final=final),
        grid=(b, s // tm),
        in_specs=[
            pl.BlockSpec((None, tm, d), row),
            pl.BlockSpec((None, 1, d), fixed3),
            pl.BlockSpec((None, d, D_FF), fixed3, pipeline_mode=pl.Buffered(1)),
            pl.BlockSpec((None, D_FF, d), fixed3, pipeline_mode=pl.Buffered(1)),
            pl.BlockSpec((1, d), lambda i, j: (0, 0)),
        ],
        out_specs=pl.BlockSpec((None, tm, d), row),
        out_shape=jax.ShapeDtypeStruct((b, s, d), F32),
        compiler_params=_params(("arbitrary", "arbitrary")),
        name="mlp_block",
    )(x, norm_mlp, w_up, w_down, norm_final)


def kernel(x, mem, norm_mix, norm_xattn, norm_mem, norm_mlp, norm_final, w_in, b_in, hgrn_lb_logits, hgrn_norm,
           mlstm_conv_w, mlstm_conv_b, mlstm_norm, w_out, w_xq, w_xkv, w_xo, w_up, w_down):
    depth = w_in.shape[0]
    b, s, d = x.shape
    assert d == D_MODEL and s % (2 * UNROLL * CHUNK) == 0 and s % min(ROW_TILE, s) == 0
    assert w_in.shape[-1] == MAIN_COLS + 4 * HEADS

    row3 = lambda t: t.reshape(depth, 1, t.shape[-1])
    w_gate = w_in[:, :, MAIN_COLS:].reshape(depth, d, 4, HEADS).transpose(0, 3, 2, 1)
    w_gate = jnp.pad(w_gate, ((0, 0), (0, 0), (0, GATE_ROWS - 4), (0, 0))).reshape(depth, HEADS * GATE_ROWS, d)
    b_gate = b_in[:, MAIN_COLS:].reshape(depth, 4, HEADS).transpose(0, 2, 1)
    b_gate = jnp.pad(b_gate, ((0, 0), (0, 0), (0, GATE_ROWS - 4))).reshape(depth, HEADS * GATE_ROWS, 1)
    w_main = w_in.astype(BF16)
    b_main = row3(b_in)
    w_gate = w_gate.astype(BF16)
    w_out_b, w_xq_b, w_xkv_b, w_xo_b = (t.astype(BF16) for t in (w_out, w_xq, w_xkv, w_xo))
    w_up_b, w_down_b = w_up.astype(BF16), w_down.astype(BF16)
    norm_mix3, norm_xattn3, norm_mlp3 = row3(norm_mix), row3(norm_xattn), row3(norm_mlp)
    hgrn_norm3, mlstm_norm3, conv_b3 = row3(hgrn_norm), row3(mlstm_norm), row3(mlstm_conv_b)
    norm_final2 = norm_final.reshape(1, d)

    lb = _lower_bounds(hgrn_lb_logits)
    kv = _mem_kv(mem, norm_mem, w_xkv_b)

    for l in range(depth):
        (hq, hkf, hkb, hcf, hcb, hv, hg, mqk, mv, mo, gates) = _in_proj(
            x, l, norm_mix3, w_main, b_main, w_gate, b_gate, lb)
        hmix, mmix = _mixer_scan(hq, hkf, hkb, hcf, hcb, hv, hg, hgrn_norm3,
                                 mqk, mlstm_conv_w, conv_b3, mv, mo, gates, mlstm_norm3, l)
        x = _attn_block(x, hmix, mmix, w_out_b, norm_xattn3, w_xq_b, kv, w_xo_b, l)
        x = _mlp_block(x, norm_mlp3, w_up_b, w_down_b, norm_final2, l, final=(l == depth - 1))
    return x
```

```python
import functools

import jax
import jax.numpy as jnp
from jax import lax
from jax.experimental import pallas as pl
from jax.experimental.pallas import tpu as pltpu

F32 = jnp.float32
BF16 = jnp.bfloat16

D_MODEL = 1024
HEADS = 4
HEAD_DIM = 128
GROUP_WIDTH = HEADS * HEAD_DIM
CONV_WIDTH = 5
XATTN_HEAD_DIM = D_MODEL // HEADS
D_FF = 4 * D_MODEL
CHUNK = 64
SUB = 8
DECAY_TERMS = 3
GATE_ROWS = 8
UNROLL = 8
STAGE_SKEW = 1
NORM_EPS = 1e-6
LOG2E = 1.4426950408889634
MAIN_COLS = 9 * GROUP_WIDTH
CONV_HALO = 16
CONV_WIN = CHUNK + 2 * CONV_HALO
ROW_TILE = 512
WIDE_ROW_TILE = 1024
VMEM_LIMIT = 56 * 1024 * 1024


def _dot(a, b):
    return jnp.dot(a, b, preferred_element_type=F32)


def _dot_nt(a, b):
    return lax.dot_general(a, b, (((1,), (1,)), ((), ())), preferred_element_type=F32)


def _dot_tn(a, b):
    return lax.dot_general(a, b, (((0,), (0,)), ((), ())), preferred_element_type=F32)


def _sigmoid(z):
    return 1.0 / (1.0 + jnp.exp(-z))


def _log_sigmoid(z):
    return jnp.minimum(z, 0.0) - jnp.log(1.0 + jnp.exp(-jnp.abs(z)))


def _rms_scale(x, gain):
    return x * lax.rsqrt(jnp.mean(x * x, axis=-1, keepdims=True) + NORM_EPS) * gain


def _split_rows(x, terms):
    parts, rest = [], x
    for i in range(terms):
        part = rest.astype(BF16)
        parts.append(part)
        if i + 1 < terms:
            rest = rest - part.astype(F32)
    return jnp.concatenate(parts, axis=0)


def _tri(reverse, terms):
    r = lax.broadcasted_iota(jnp.int32, (CHUNK, terms * CHUNK), 0)
    s = lax.broadcasted_iota(jnp.int32, (CHUNK, terms * CHUNK), 1) % CHUNK
    return ((s >= r) if reverse else (s <= r)).astype(BF16)


def _params(semantics):
    return pltpu.CompilerParams(dimension_semantics=semantics, vmem_limit_bytes=VMEM_LIMIT)


def _lb_kernel(logit_ref, lb_ref, *, depth):
    for d in range(2):
        rows = [logit_ref[d * depth + l:d * depth + l + 1, :] for l in range(depth)]
        mx = functools.reduce(jnp.maximum, rows)
        es = [jnp.exp(r - mx) for r in rows]
        tot = functools.reduce(lambda a, b: a + b, es)
        c = None
        c0 = None
        for l in range(depth):
            p = es[l] / tot
            c = p if c is None else c + p
            if l == 0:
                c0 = c
            lb_ref[2 * l + d:2 * l + d + 1, :] = c - c0


def _lower_bounds(logits):
    _, depth, width = logits.shape
    out = pl.pallas_call(
        functools.partial(_lb_kernel, depth=depth),
        out_shape=jax.ShapeDtypeStruct((2 * depth, width), F32),
        name="lower_bounds",
    )(logits.astype(F32).reshape(2 * depth, width))
    return out.reshape(depth, 2, width)


def _mem_kv_kernel(mem_ref, g_ref, w_ref, kv_ref):
    memn = _rms_scale(mem_ref[...], g_ref[...]).astype(BF16)
    kv_ref[...] = _dot(memn, w_ref[...]).astype(BF16)


def _mem_kv(mem, norm_mem, w_xkv):
    b, m, d = mem.shape
    depth = norm_mem.shape[0]
    return pl.pallas_call(
        _mem_kv_kernel,
        grid=(depth, b),
        in_specs=[
            pl.BlockSpec((None, m, d), lambda l, i: (i, 0, 0)),
            pl.BlockSpec((None, 1, d), lambda l, i: (l, 0, 0)),
            pl.BlockSpec((None, d, 2 * d), lambda l, i: (l, 0, 0)),
        ],
        out_specs=pl.BlockSpec((None, None, m, 2 * d), lambda l, i: (l, i, 0, 0)),
        out_shape=jax.ShapeDtypeStruct((depth, b, m, 2 * d), BF16),
        compiler_params=_params(("arbitrary", "arbitrary")),
        name="mem_kv",
    )(mem, norm_mem.reshape(depth, 1, d), w_xkv)


def _in_proj_kernel(x_ref, g_ref, w_ref, b_ref, wg_ref, bg_ref, lb_ref, tri_ref,
                    hq_ref, hkf_ref, hkb_ref, hcf_ref, hcb_ref, hv_ref, hg_ref,
                    mqk_ref, mv_ref, mo_ref, gate_ref):
    w = GROUP_WIDTH
    c = CHUNK
    n_chunks = x_ref.shape[0] // c
    xn = _rms_scale(x_ref[...], g_ref[...]).astype(BF16)
    tri_f, tri_b = _tri(False, DECAY_TERMS), _tri(True, DECAY_TERMS)

    def proj(group):
        c0 = group * w
        return _dot(xn, w_ref[:, c0:c0 + w]) + b_ref[:, c0:c0 + w]

    def silu_to(z, ref):
        ref[...] = (z * _sigmoid(z)).astype(BF16)

    def forget_rows(z, lb, tri, k_ref, cum_ref, r0, r1):
        z = z[r0:r1]
        e = jnp.exp(-jnp.abs(z))
        log_sig = jnp.minimum(z, 0.0) - jnp.log(1.0 + e)
        a = jnp.log(lb)
        t = jnp.log1p(-lb) + log_sig
        lf2 = (jnp.maximum(a, t) + jnp.log(1.0 + jnp.exp(-jnp.abs(a - t)))) * LOG2E
        k_ref[r0:r1, :] = ((1.0 - lb) * (jnp.where(z >= 0, e, 1.0) / (1.0 + e))).astype(BF16)
        for j in range((r1 - r0) // c):
            cum_ref[r0 + j * c:r0 + (j + 1) * c, :] = _dot(tri, _split_rows(lf2[j * c:(j + 1) * c], DECAY_TERMS))

    half = x_ref.shape[0] // 2
    lb_f, lb_b = lb_ref[0:1, :], lb_ref[1:2, :]
    z_q, z_ff = proj(0), proj(1)
    z_fb = proj(2)
    silu_to(z_q, hq_ref)
    z_v = proj(3)
    forget_rows(z_ff, lb_f, tri_f, hkf_ref, hcf_ref, 0, half)
    z_g = proj(4)
    forget_rows(z_ff, lb_f, tri_f, hkf_ref, hcf_ref, half, 2 * half)
    hv_ref[...] = z_v.astype(BF16)
    z_mq = proj(5)
    forget_rows(z_fb, lb_b, tri_b, hkb_ref, hcb_ref, 0, half)
    z_mk = proj(6)
    forget_rows(z_fb, lb_b, tri_b, hkb_ref, hcb_ref, half, 2 * half)
    mqk_ref[:, 0:w] = z_mq.astype(BF16)
    z_mv = proj(7)
    silu_to(z_g, hg_ref)
    mqk_ref[:, w:2 * w] = z_mk.astype(BF16)
    z_mo = proj(8)
    mv_ref[...] = z_mv.astype(BF16)
    mo_ref[...] = _sigmoid(z_mo).astype(BF16)
    tm = x_ref.shape[0]
    zg = _dot_nt(wg_ref[...], xn) + bg_ref[...]
    kind = lax.broadcasted_iota(jnp.int32, zg.shape, 0) % GATE_ROWS
    g2 = jnp.where(kind >= 2, _log_sigmoid(zg), zg) * LOG2E
    g3 = _split_rows(g2, 3)

    def lane_cumsum(tri):
        parts = _dot(g3, tri)
        n = zg.shape[0]
        return parts[0:n] + parts[n:2 * n] + parts[2 * n:3 * n]

    prefix, suffix = lane_cumsum(tri_ref[0]), lane_cumsum(tri_ref[1])
    rows = jnp.where(kind == 2, prefix, jnp.where(kind == 3, suffix, g2))
    for j in range(n_chunks):
        for h in range(HEADS):
            gate_ref[j, h] = rows[h * GATE_ROWS:(h + 1) * GATE_ROWS, j * c:(j + 1) * c]


def _in_proj(x, layer, norm_mix, w_main, b_main, w_gate, b_gate, lb):
    b, s, d = x.shape
    tm = min(ROW_TILE, s)
    w = GROUP_WIDTH
    grid = (b, s // tm)
    row = lambda i, j: (i, j, 0)
    fixed3 = lambda i, j: (layer, 0, 0)
    act = lambda width, dtype: jax.ShapeDtypeStruct((b, s, width), dtype)
    out_shapes = [act(w, BF16), act(w, BF16), act(w, BF16), act(w, F32), act(w, F32), act(w, BF16),
                  act(w, BF16), act(2 * w, BF16), act(w, BF16), act(w, BF16)]
    out_specs = [pl.BlockSpec((None, tm, o.shape[-1]), row) for o in out_shapes]
    pos = jnp.arange(tm)
    same_chunk = (pos[:, None] // CHUNK) == (pos[None, :] // CHUNK)
    tri = jnp.stack([same_chunk & (pos[:, None] <= pos[None, :]),
                     same_chunk & (pos[:, None] >= pos[None, :])]).astype(BF16)
    out_shapes.append(jax.ShapeDtypeStruct((b, s // CHUNK, HEADS, GATE_ROWS, CHUNK), F32))
    out_specs.append(pl.BlockSpec((None, tm // CHUNK, HEADS, GATE_ROWS, CHUNK), lambda i, j: (i, j, 0, 0, 0)))
    return pl.pallas_call(
        _in_proj_kernel,
        grid=grid,
        in_specs=[
            pl.BlockSpec((None, tm, d), row),
            pl.BlockSpec((None, 1, d), fixed3),
            pl.BlockSpec((None, d, MAIN_COLS), fixed3, pipeline_mode=pl.Buffered(1)),
            pl.BlockSpec((None, 1, MAIN_COLS), fixed3),
            pl.BlockSpec((None, HEADS * GATE_ROWS, d), fixed3),
            pl.BlockSpec((None, HEADS * GATE_ROWS, 1), fixed3),
            pl.BlockSpec((None, 2, w), fixed3),
            pl.BlockSpec((2, tm, tm), lambda i, j: (0, 0, 0)),
        ],
        out_specs=out_specs,
        out_shape=out_shapes,
        compiler_params=_params(("arbitrary", "arbitrary")),
        name="in_proj",
    )(x, norm_mix, w_main, b_main, w_gate, b_gate, lb, tri)


def _hgrn_offdiag(q, k, cum, reverse):
    c = CHUNK
    r64 = lax.broadcasted_iota(jnp.int32, (c, c), 0)
    c64 = lax.broadcasted_iota(jnp.int32, (c, c), 1)
    a = None
    width = c // 2
    while width >= SUB:
        span = 2 * width
        expo = []
        for base in range(0, c, span):
            r = base + width if reverse else base + width - 1
            ref, lo, hi = cum[r:r + 1], cum[base:base + width], cum[base + width:base + span]
            expo += [lo - ref, ref - hi] if reverse else [ref - lo, hi - ref]
        e = jnp.exp2(jnp.concatenate(expo, axis=0))
        qe, ke = q * e, k * e
        zeros = jnp.zeros((width, HEAD_DIM), F32)
        qparts, kparts = [], []
        for base in range(0, c, span):
            lo, hi = slice(base, base + width), slice(base + width, base + span)
            if reverse:
                qparts += [qe[lo], zeros]
                kparts += [zeros, ke[hi]]
            else:
                qparts += [zeros, qe[hi]]
                kparts += [ke[lo], zeros]
        a_l = _dot_nt(jnp.concatenate(qparts, axis=0).astype(BF16), jnp.concatenate(kparts, axis=0).astype(BF16))
        if span < c:
            a_l = jnp.where((r64 ^ c64) < span, a_l, 0.0)
        a = a_l if a is None else a + a_l
        width //= 2
    return a


def _hgrn_diag(q, cum, c2_ref, sel_ref, mask):
    rows = []
    for b0 in range(0, CHUNK, SUB):
        cb, qb = cum[b0:b0 + SUB], q[b0:b0 + SUB]
        tiles = []
        for s in range(SUB):
            c2s = c2_ref[b0 + s:b0 + s + 1, :]
            tiles.append(qb * jnp.exp2(jnp.minimum(cb - c2s, 0.0)))
        rows.append(jnp.concatenate(tiles, axis=1))
    return _dot(jnp.concatenate(rows, axis=0).astype(BF16), sel_ref[...]) * mask


def _hgrn_diag_tables(sel_ref, mask_ref):
    k = lax.broadcasted_iota(jnp.int32, sel_ref.shape, 0)
    lane = lax.broadcasted_iota(jnp.int32, sel_ref.shape, 1)
    sel_ref[...] = (k // HEAD_DIM == lane % SUB).astype(BF16)
    t = lax.broadcasted_iota(jnp.int32, (CHUNK, HEAD_DIM), 0)
    s = lax.broadcasted_iota(jnp.int32, (CHUNK, HEAD_DIM), 1)
    same_block = (t // SUB) == (s // SUB)
    mask_ref[0] = (same_block & (s <= t)).astype(F32)
    mask_ref[1] = (same_block & (s >= t)).astype(F32)


def _finish_rows(rows, o_scr, gate_ref, gain_ref, out_ref):
    total = o_scr[rows, :]
    sq = _dot((total * total).astype(BF16), jnp.ones((HEAD_DIM, HEAD_DIM), BF16))
    yield
    y = total * lax.rsqrt(sq * (1.0 / HEAD_DIM) + NORM_EPS) * gain_ref[...] * gate_ref[rows, :].astype(F32)
    out_ref[rows, :] = y.astype(out_ref.dtype)


def _hgrn_chunk_stages(q_ref, k_ref, cum_ref, v_ref, rows, c2_ref, st_ref, o_scr, reverse, accumulate,
                       sel_ref, mask_ref):
    c = CHUNK
    q, k = q_ref[rows, :].astype(F32), k_ref[rows, :].astype(F32)
    cum, v = cum_ref[rows, :], v_ref[rows, :]
    c2_ref[...] = cum - jnp.log(k) * LOG2E
    q_in = (q * jnp.exp2(cum)).astype(BF16)
    last = cum[0:1] if reverse else cum[c - 1:c]
    kv_new = _dot_tn(v, (k * jnp.exp2(last - cum)).astype(BF16))
    a_off = _hgrn_offdiag(q, k, cum, reverse)
    diag = _hgrn_diag(q, cum, c2_ref, sel_ref, mask_ref[int(reverse)])
    yield
    o_intra = _dot((a_off + diag[:, 0:c]).astype(BF16), v)
    yield
    st = st_ref[...]
    o_inter = _dot_nt(q_in, st.astype(BF16))
    st_ref[...] = st * jnp.exp2(last) + kv_new
    yield
    if accumulate:
        o_scr[rows, :] += o_inter + o_intra
    else:
        o_scr[rows, :] = o_inter + o_intra


def _delayed(rounds, stages):
    for _ in range(rounds):
        yield
    yield from stages


def _run_interleaved(stages):
    alive = list(stages)
    while alive:
        still = []
        for g in alive:
            try:
                next(g)
                still.append(g)
            except StopIteration:
                pass
        alive = still


def _scan_loops(nc, step):
    groups = nc // UNROLL
    lax.fori_loop(0, groups // 2, lambda i, carry: (step(i * UNROLL, False), carry)[1], 0)
    lax.fori_loop(groups // 2, groups, lambda i, carry: (step(i * UNROLL, True), carry)[1], 0)


def _chunk_index(nc, first, u, reverse):
    return (nc - 1 - first - u) if reverse else first + u


def _chunk_rows(nc, first, u, reverse):
    start = _chunk_index(nc, first, u, reverse) * CHUNK
    return pl.ds(start if isinstance(start, int) else pl.multiple_of(start, CHUNK), CHUNK)


def _mlstm_chunk_stages(q_scr, k_scr, v_ref, gates_ref, ci, st_ref, n_ref, m_ref, o_scr, reverse, accumulate):
    c = CHUNK
    rows = pl.ds(pl.multiple_of(ci * c, c), c)
    ig_row, cum_row = (1, 3) if reverse else (0, 2)
    q, k = q_scr[rows, :], k_scr[rows, :]
    v_ext = jnp.concatenate([v_ref[rows, :], jnp.ones((c, HEAD_DIM), BF16)], axis=1)
    g_rows = gates_ref[ci]
    hi = g_rows.astype(BF16).astype(F32)
    mid = (g_rows - hi).astype(BF16).astype(F32)
    lo = g_rows - hi - mid
    g3 = jnp.concatenate([hi, mid, lo, jnp.zeros_like(lo)], axis=0).astype(BF16)
    part = lax.broadcasted_iota(jnp.int32, (4 * GATE_ROWS, 2 * HEAD_DIM), 0)
    want = jnp.where(lax.broadcasted_iota(jnp.int32, part.shape, 1) < HEAD_DIM, ig_row, cum_row)
    cols = _dot_tn(g3, ((part % GATE_ROWS == want) & (part < 3 * GATE_ROWS)).astype(BF16))
    ig_c, cum_c = cols[:, 0:HEAD_DIM], cols[:, HEAD_DIM:2 * HEAD_DIM]
    qk = _dot_nt(q, k)
    yield
    ig_r, cum_r = g_rows[ig_row:ig_row + 1, :], g_rows[cum_row:cum_row + 1, :]
    t_idx = lax.broadcasted_iota(jnp.int32, (c, c), 0)
    s_idx = lax.broadcasted_iota(jnp.int32, (c, c), 1)
    causal = (s_idx >= t_idx) if reverse else (s_idx <= t_idx)
    log_d = jnp.where(causal, cum_c[:, 0:c] - cum_r + ig_r, -jnp.inf)
    row_max = jnp.max(log_d, axis=-1, keepdims=True)
    last = cum_c[0:1] if reverse else cum_c[c - 1:c]
    log_w = last - cum_c + ig_c
    w_max = jnp.max(log_w, axis=0, keepdims=True)
    yield
    m = m_ref[...]
    log_inter = cum_c + m
    m_t = jnp.maximum(row_max, log_inter)
    w_inter = jnp.exp2(log_inter - m_t)
    scores = qk * jnp.exp2(log_d - m_t[:, 0:c])
    intra = _dot(scores.astype(BF16), v_ext)
    m_new = jnp.maximum(last + m, w_max)
    m_ref[...] = m_new
    carry = jnp.exp2(last + m - m_new)
    k_w = (k.astype(F32) * jnp.exp2(log_w - m_new)).astype(BF16)
    kv_new = _dot_tn(v_ext[:, 0:HEAD_DIM], k_w)
    n_new = _dot(jnp.ones((n_ref.shape[0], c), BF16), k_w)
    yield
    st, n = st_ref[...], n_ref[...]
    n_rep = jnp.broadcast_to(n[0:1, :].astype(BF16), (HEAD_DIM, HEAD_DIM))
    inter = _dot_nt(q, jnp.concatenate([st.astype(BF16), n_rep], axis=0))
    st_ref[...] = carry * st + kv_new
    n_ref[...] = carry * n + n_new
    yield
    num = intra[:, 0:HEAD_DIM] + w_inter * inter[:, 0:HEAD_DIM]
    den = intra[:, HEAD_DIM:2 * HEAD_DIM] + w_inter * inter[:, HEAD_DIM:2 * HEAD_DIM]
    h_out = num / jnp.maximum(jnp.abs(den), jnp.exp2(-m_t))
    if accumulate:
        o_scr[rows, :] += h_out
    else:
        o_scr[rows, :] = h_out


def _mlstm_conv(mq_ref, mk_ref, cwq_ref, cwk_ref, cbq_ref, cbk_ref, pad_scr, q_scr, k_scr, seq):
    c = CHUNK
    hd = HEAD_DIM
    halo = CONV_HALO
    pad_scr[0:halo, :] = jnp.zeros((halo, 2 * hd), BF16)
    pad_scr[halo + seq:2 * halo + seq, :] = jnp.zeros((halo, 2 * hd), BF16)
    pad_scr[halo:halo + seq, 0:hd] = mq_ref[...]
    pad_scr[halo:halo + seq, hd:2 * hd] = mk_ref[...]
    out_i = lax.broadcasted_iota(jnp.int32, (c, CONV_WIDTH * CONV_WIN), 0)
    col_i = lax.broadcasted_iota(jnp.int32, (c, CONV_WIDTH * CONV_WIN), 1)
    shifts = (col_i % CONV_WIN == out_i + col_i // CONV_WIN - CONV_WIDTH // 2 + halo).astype(BF16)
    taps = jnp.concatenate([cwq_ref[...], cwk_ref[...]], axis=1).astype(BF16)
    bias = jnp.concatenate([cbq_ref[...], cbk_ref[...]], axis=1)

    def conv_body(i, carry):
        start = pl.multiple_of(i * c, c)
        win = pad_scr[pl.ds(start, CONV_WIN), :]
        scaled = jnp.concatenate([win * taps[j:j + 1, :] for j in range(CONV_WIDTH)], axis=0)
        y = bias + _dot(shifts, scaled)
        y = y * _sigmoid(y)
        q_scr[pl.ds(start, c), :] = y[:, 0:hd].astype(BF16)
        k_scr[pl.ds(start, c), :] = (y[:, hd:2 * hd] * (hd ** -0.5)).astype(BF16)
        return carry

    lax.fori_loop(0, seq // c, conv_body, 0, unroll=8)


def _mixer_kernel(hq_ref, hkf_ref, hkb_ref, hcf_ref, hcb_ref, hv_ref, hg_ref, hgain_ref,
                  mq_ref, mk_ref, cwq_ref, cwk_ref, cbq_ref, cbk_ref, mv_ref, mo_ref, gates_ref, mgain_ref,
                  hout_ref, mout_ref,
                  ho_scr, c2_scr, hstf_scr, hstb_scr, sel_scr, mask_scr,
                  pad_scr, q_scr, k_scr, mo_scr, mstf_scr, mstb_scr, mnf_scr, mnb_scr, mf_scr, mb_scr, *, seq):
    nc = seq // CHUNK
    _mlstm_conv(mq_ref, mk_ref, cwq_ref, cwk_ref, cbq_ref, cbk_ref, pad_scr, q_scr, k_scr, seq)
    for ref in (hstf_scr, hstb_scr, mstf_scr, mstb_scr, mnf_scr, mnb_scr, mf_scr, mb_scr):
        ref[...] = jnp.zeros_like(ref)
    _hgrn_diag_tables(sel_scr, mask_scr)

    def finish_stages(first):
        stages = []
        for u in range(UNROLL):
            for reverse in (False, True):
                rows = _chunk_rows(nc, first, u, reverse)
                stages.append(_finish_rows(rows, ho_scr, hg_ref, hgain_ref, hout_ref))
                stages.append(_finish_rows(rows, mo_scr, mo_ref, mgain_ref, mout_ref))
        return stages

    def step(first, accumulate):
        stages = finish_stages(first - UNROLL) if accumulate else []
        for u in range(UNROLL):
            for reverse in (False, True):
                stages.append(_delayed(u * STAGE_SKEW, _hgrn_chunk_stages(
                    hq_ref, hkb_ref if reverse else hkf_ref, hcb_ref if reverse else hcf_ref, hv_ref,
                    _chunk_rows(nc, first, u, reverse), c2_scr.at[2 * u + int(reverse)],
                    hstb_scr if reverse else hstf_scr, ho_scr, reverse, accumulate, sel_scr, mask_scr)))
                stages.append(_delayed(u * STAGE_SKEW, _mlstm_chunk_stages(
                    q_scr, k_scr, mv_ref, gates_ref, _chunk_index(nc, first, u, reverse),
                    mstb_scr if reverse else mstf_scr, mnb_scr if reverse else mnf_scr,
                    mb_scr if reverse else mf_scr, mo_scr, reverse, accumulate)))
        _run_interleaved(stages)

    _scan_loops(nc, step)
    _run_interleaved(finish_stages(nc - UNROLL))


def _mixer_scan(hq, hkf, hkb, hcf, hcb, hv, hg, hgain, mqk, conv_w, conv_b, mv, mo, gates, mgain, layer):
    b, s, _ = hq.shape
    hd = HEAD_DIM
    blk = lambda off=0: pl.BlockSpec((None, s, hd), lambda i, h: (i, 0, h + off))
    par = lambda rows, off=0: pl.BlockSpec((None, rows, hd), lambda i, h: (layer, 0, h + off))
    out = jax.ShapeDtypeStruct((b, s, GROUP_WIDTH), BF16)
    return pl.pallas_call(
        functools.partial(_mixer_kernel, seq=s),
        grid=(b, HEADS),
        in_specs=[blk(), blk(), blk(), blk(), blk(), blk(), blk(), par(1),
                  blk(), blk(HEADS), par(CONV_WIDTH), par(CONV_WIDTH, HEADS), par(1), par(1, HEADS),
                  blk(), blk(),
                  pl.BlockSpec((None, s // CHUNK, None, GATE_ROWS, CHUNK), lambda i, h: (i, 0, h, 0, 0)),
                  par(1)],
        out_specs=[blk(), blk()],
        out_shape=[out, out],
        scratch_shapes=[pltpu.VMEM((s, hd), F32), pltpu.VMEM((2 * UNROLL, CHUNK, hd), F32),
                        pltpu.VMEM((hd, hd), F32), pltpu.VMEM((hd, hd), F32),
                        pltpu.VMEM((SUB * hd, hd), BF16), pltpu.VMEM((2, CHUNK, hd), F32),
                        pltpu.VMEM((s + 2 * CONV_HALO, 2 * hd), BF16),
                        pltpu.VMEM((s, hd), BF16), pltpu.VMEM((s, hd), BF16), pltpu.VMEM((s, hd), F32),
                        pltpu.VMEM((hd, hd), F32), pltpu.VMEM((hd, hd), F32),
                        pltpu.VMEM((8, hd), F32), pltpu.VMEM((8, hd), F32),
                        pltpu.VMEM((1, hd), F32), pltpu.VMEM((1, hd), F32)],
        compiler_params=_params(("arbitrary", "arbitrary")),
        name="mixer_scan",
    )(hq, hkf, hkb, hcf, hcb, hv, hg, hgain, mqk, mqk, conv_w, conv_w, conv_b, conv_b, mv, mo, gates, mgain)


def _attn_kernel(x_ref, hmix_ref, mmix_ref, wout_ref, g_ref, wq_ref, kv_ref, wo_ref, out_ref):
    w = GROUP_WIDTH
    hd = XATTN_HEAD_DIM
    x1 = x_ref[...] + _dot(hmix_ref[...], wout_ref[0:w, :]) + _dot(mmix_ref[...], wout_ref[w:2 * w, :])
    xn = _rms_scale(x1, g_ref[...]).astype(BF16)
    q = (_dot(xn, wq_ref[...]) * (hd ** -0.5)).astype(BF16)
    heads = []
    for h in range(HEADS):
        kh = kv_ref[:, h * hd:(h + 1) * hd]
        vh = kv_ref[:, D_MODEL + h * hd:D_MODEL + (h + 1) * hd]
        sc = _dot_nt(q[:, h * hd:(h + 1) * hd], kh)
        p = jnp.exp(sc - jnp.max(sc, axis=-1, keepdims=True))
        o = _dot(p.astype(BF16), vh) / jnp.sum(p, axis=-1, keepdims=True)
        heads.append(o.astype(BF16))
    out_ref[...] = x1 + _dot(jnp.concatenate(heads, axis=1), wo_ref[...])


def _attn_block(x, hmix, mmix, w_out, norm_xattn, w_xq, kv, w_xo, layer):
    b, s, d = x.shape
    tm = min(WIDE_ROW_TILE, s)
    m = kv.shape[2]
    row = lambda i, j: (i, j, 0)
    fixed3 = lambda i, j: (layer, 0, 0)
    wspec = lambda: pl.BlockSpec((None, d, d), fixed3, pipeline_mode=pl.Buffered(1))
    return pl.pallas_call(
        _attn_kernel,
        grid=(b, s // tm),
        in_specs=[
            pl.BlockSpec((None, tm, d), row),
            pl.BlockSpec((None, tm, GROUP_WIDTH), row),
            pl.BlockSpec((None, tm, GROUP_WIDTH), row),
            wspec(),
            pl.BlockSpec((None, 1, d), fixed3),
            wspec(),
            pl.BlockSpec((None, None, m, 2 * d), lambda i, j: (layer, i, 0, 0)),
            wspec(),
        ],
        out_specs=pl.BlockSpec((None, tm, d), row),
        out_shape=jax.ShapeDtypeStruct((b, s, d), F32),
        compiler_params=_params(("arbitrary", "arbitrary")),
        name="attn_block",
    )(x, hmix, mmix, w_out, norm_xattn, w_xq, kv, w_xo)


def _mlp_kernel(x_ref, g_ref, wup_ref, wdn_ref, gfin_ref, out_ref, *, final):
    x = x_ref[...]
    xn = _rms_scale(x, g_ref[...]).astype(BF16)
    acc = x
    step = D_MODEL
    for c0 in range(0, D_FF, step):
        hcol = jnp.maximum(_dot(xn, wup_ref[:, c0:c0 + step]), 0.0)
        acc = acc + _dot((hcol * hcol).astype(BF16), wdn_ref[c0:c0 + step, :])
    if final:
        acc = _rms_scale(acc, gfin_ref[...])
    out_ref[...] = acc


def _mlp_block(x, norm_mlp, w_up, w_down, norm_final, layer, final):
    b, s, d = x.shape
    tm = min(WIDE_ROW_TILE, s)
    row = lambda i, j: (i, j, 0)
    fixed3 = lambda i, j: (layer, 0, 0)
    return pl.pallas_call(
        functools.partial(_mlp_kernel, final=final),
        grid=(b, s // tm),
        in_specs=[
            pl.BlockSpec((None, tm, d), row),
            pl.BlockSpec((None, 1, d), fixed3),
            pl.BlockSpec((None, d, D_FF), fixed3, pipeline_mode=pl.Buffered(1)),
            pl.BlockSpec((None, D_FF, d), fixed3, pipeline_mode=pl.Buffered(1)),
            pl.BlockSpec((1, d), lambda i, j: (0, 0)),
        ],
        out_specs=pl.BlockSpec((None, tm, d), row),
        out_shape=jax.ShapeDtypeStruct((b, s, d), F32),
        compiler_params=_params(("arbitrary", "arbitrary")),
        name="mlp_block",
    )(x, norm_mlp, w_up, w_down, norm_final)


def kernel(x, mem, norm_mix, norm_xattn, norm_mem, norm_mlp, norm_final, w_in, b_in, hgrn_lb_logits, hgrn_norm,
           mlstm_conv_w, mlstm_conv_b, mlstm_norm, w_out, w_xq, w_xkv, w_xo, w_up, w_down):
    depth = w_in.shape[0]
    b, s, d = x.shape
    assert d == D_MODEL and s % (2 * UNROLL * CHUNK) == 0 and s % min(WIDE_ROW_TILE, s) == 0
    assert w_in.shape[-1] == MAIN_COLS + 4 * HEADS

    row3 = lambda t: t.reshape(depth, 1, t.shape[-1])
    w_gate = w_in[:, :, MAIN_COLS:].reshape(depth, d, 4, HEADS).transpose(0, 3, 2, 1)
    w_gate = jnp.pad(w_gate, ((0, 0), (0, 0), (0, GATE_ROWS - 4), (0, 0))).reshape(depth, HEADS * GATE_ROWS, d)
    b_gate = b_in[:, MAIN_COLS:].reshape(depth, 4, HEADS).transpose(0, 2, 1)
    b_gate = jnp.pad(b_gate, ((0, 0), (0, 0), (0, GATE_ROWS - 4))).reshape(depth, HEADS * GATE_ROWS, 1)
    w_main = w_in.astype(BF16)
    b_main = row3(b_in)
    w_gate = w_gate.astype(BF16)
    w_out_b, w_xq_b, w_xkv_b, w_xo_b = (t.astype(BF16) for t in (w_out, w_xq, w_xkv, w_xo))
    w_up_b, w_down_b = w_up.astype(BF16), w_down.astype(BF16)
    norm_mix3, norm_xattn3, norm_mlp3 = row3(norm_mix), row3(norm_xattn), row3(norm_mlp)
    hgrn_norm3, mlstm_norm3, conv_b3 = row3(hgrn_norm), row3(mlstm_norm), row3(mlstm_conv_b)
    norm_final2 = norm_final.reshape(1, d)

    lb = _lower_bounds(hgrn_lb_logits)
    kv = _mem_kv(mem, norm_mem, w_xkv_b)

    for l in range(depth):
        (hq, hkf, hkb, hcf, hcb, hv, hg, mqk, mv, mo, gates) = _in_proj(
            x, l, norm_mix3, w_main, b_main, w_gate, b_gate, lb)
        hmix, mmix = _mixer_scan(hq, hkf, hkb, hcf, hcb, hv, hg, hgrn_norm3,
                                 mqk, mlstm_conv_w, conv_b3, mv, mo, gates, mlstm_norm3, l)
        x = _attn_block(x, hmix, mmix, w_out_b, norm_xattn3, w_xq_b, kv, w_xo_b, l)
        x = _mlp_block(x, norm_mlp3, w_up_b, w_down_b, norm_final2, l, final=(l == depth - 1))
    return x
```

```python
import functools

import jax
import jax.numpy as jnp
from jax import lax
from jax.experimental import pallas as pl
from jax.experimental.pallas import tpu as pltpu

F32 = jnp.float32
BF16 = jnp.bfloat16

D_MODEL = 1024
HEADS = 4
HEAD_DIM = 128
GROUP_WIDTH = HEADS * HEAD_DIM
CONV_WIDTH = 5
XATTN_HEAD_DIM = D_MODEL // HEADS
D_FF = 4 * D_MODEL
CHUNK = 64
SUB = 8
DECAY_TERMS = 3
GATE_ROWS = 8
UNROLL = 8
STAGE_SKEW = 1
NORM_EPS = 1e-6
LOG2E = 1.4426950408889634
MAIN_COLS = 9 * GROUP_WIDTH
CONV_HALO = 16
CONV_WIN = CHUNK + 2 * CONV_HALO
ROW_TILE = 512
WIDE_ROW_TILE = 1024
VMEM_LIMIT = 56 * 1024 * 1024


def _dot(a, b):
    return jnp.dot(a, b, preferred_element_type=F32)


def _dot_nt(a, b):
    return lax.dot_general(a, b, (((1,), (1,)), ((), ())), preferred_element_type=F32)


def _dot_tn(a, b):
    return lax.dot_general(a, b, (((0,), (0,)), ((), ())), preferred_element_type=F32)


def _sigmoid(z):
    return 1.0 / (1.0 + jnp.exp(-z))


def _log_sigmoid(z):
    return jnp.minimum(z, 0.0) - jnp.log(1.0 + jnp.exp(-jnp.abs(z)))


def _rms_scale(x, gain):
    return x * lax.rsqrt(jnp.mean(x * x, axis=-1, keepdims=True) + NORM_EPS) * gain


def _split_rows(x, terms):
    parts, rest = [], x
    for i in range(terms):
        part = rest.astype(BF16)
        parts.append(part)
        if i + 1 < terms:
            rest = rest - part.astype(F32)
    return jnp.concatenate(parts, axis=0)


def _tri(reverse, terms):
    r = lax.broadcasted_iota(jnp.int32, (CHUNK, terms * CHUNK), 0)
    s = lax.broadcasted_iota(jnp.int32, (CHUNK, terms * CHUNK), 1) % CHUNK
    return ((s >= r) if reverse else (s <= r)).astype(BF16)


def _params(semantics):
    return pltpu.CompilerParams(dimension_semantics=semantics, vmem_limit_bytes=VMEM_LIMIT)


def _lb_kernel(logit_ref, lb_ref, *, depth):
    for d in range(2):
        rows = [logit_ref[d * depth + l:d * depth + l + 1, :] for l in range(depth)]
        mx = functools.reduce(jnp.maximum, rows)
        es = [jnp.exp(r - mx) for r in rows]
        tot = functools.reduce(lambda a, b: a + b, es)
        c = None
        c0 = None
        for l in range(depth):
            p = es[l] / tot
            c = p if c is None else c + p
            if l == 0:
                c0 = c
            lb_ref[2 * l + d:2 * l + d + 1, :] = c - c0


def _lower_bounds(logits):
    _, depth, width = logits.shape
    out = pl.pallas_call(
        functools.partial(_lb_kernel, depth=depth),
        out_shape=jax.ShapeDtypeStruct((2 * depth, width), F32),
        name="lower_bounds",
    )(logits.astype(F32).reshape(2 * depth, width))
    return out.reshape(depth, 2, width)


def _mem_kv_kernel(mem_ref, g_ref, w_ref, kv_ref):
    memn = _rms_scale(mem_ref[...], g_ref[...]).astype(BF16)
    kv_ref[...] = _dot(memn, w_ref[...]).astype(BF16)


def _mem_kv(mem, norm_mem, w_xkv):
    b, m, d = mem.shape
    depth = norm_mem.shape[0]
    return pl.pallas_call(
        _mem_kv_kernel,
        grid=(depth, b),
        in_specs=[
            pl.BlockSpec((None, m, d), lambda l, i: (i, 0, 0)),
            pl.BlockSpec((None, 1, d), lambda l, i: (l, 0, 0)),
            pl.BlockSpec((None, d, 2 * d), lambda l, i: (l, 0, 0)),
        ],
        out_specs=pl.BlockSpec((None, None, m, 2 * d), lambda l, i: (l, i, 0, 0)),
        out_shape=jax.ShapeDtypeStruct((depth, b, m, 2 * d), BF16),
        compiler_params=_params(("arbitrary", "arbitrary")),
        name="mem_kv",
    )(mem, norm_mem.reshape(depth, 1, d), w_xkv)


def _in_proj_kernel(x_ref, g_ref, w_ref, b_ref, wg_ref, bg_ref, lb_ref,
                    hq_ref, hkf_ref, hkb_ref, hcf_ref, hcb_ref, hv_ref, hg_ref,
                    mqk_ref, mv_ref, mo_ref, gate_ref):
    w = GROUP_WIDTH
    c = CHUNK
    n_chunks = x_ref.shape[0] // c
    xn = _rms_scale(x_ref[...], g_ref[...]).astype(BF16)
    tri_f, tri_b = _tri(False, DECAY_TERMS), _tri(True, DECAY_TERMS)

    def proj(group):
        c0 = group * w
        return _dot(xn, w_ref[:, c0:c0 + w]) + b_ref[:, c0:c0 + w]

    def silu_to(z, ref):
        ref[...] = (z * _sigmoid(z)).astype(BF16)

    def forget_rows(z, lb, tri, k_ref, cum_ref, r0, r1):
        z = z[r0:r1]
        e = jnp.exp(-jnp.abs(z))
        log_sig = jnp.minimum(z, 0.0) - jnp.log(1.0 + e)
        a = jnp.log(lb)
        t = jnp.log1p(-lb) + log_sig
        lf2 = (jnp.maximum(a, t) + jnp.log(1.0 + jnp.exp(-jnp.abs(a - t)))) * LOG2E
        k_ref[r0:r1, :] = ((1.0 - lb) * (jnp.where(z >= 0, e, 1.0) / (1.0 + e))).astype(BF16)
        for j in range((r1 - r0) // c):
            cum_ref[r0 + j * c:r0 + (j + 1) * c, :] = _dot(tri, _split_rows(lf2[j * c:(j + 1) * c], DECAY_TERMS))

    half = x_ref.shape[0] // 2
    lb_f, lb_b = lb_ref[0:1, :], lb_ref[1:2, :]
    z_q, z_ff = proj(0), proj(1)
    z_fb = proj(2)
    silu_to(z_q, hq_ref)
    z_v = proj(3)
    forget_rows(z_ff, lb_f, tri_f, hkf_ref, hcf_ref, 0, half)
    z_g = proj(4)
    forget_rows(z_ff, lb_f, tri_f, hkf_ref, hcf_ref, half, 2 * half)
    hv_ref[...] = z_v.astype(BF16)
    z_mq = proj(5)
    forget_rows(z_fb, lb_b, tri_b, hkb_ref, hcb_ref, 0, half)
    z_mk = proj(6)
    forget_rows(z_fb, lb_b, tri_b, hkb_ref, hcb_ref, half, 2 * half)
    mqk_ref[:, 0:w] = z_mq.astype(BF16)
    z_mv = proj(7)
    silu_to(z_g, hg_ref)
    mqk_ref[:, w:2 * w] = z_mk.astype(BF16)
    z_mo = proj(8)
    mv_ref[...] = z_mv.astype(BF16)
    mo_ref[...] = _sigmoid(z_mo).astype(BF16)
    tm = x_ref.shape[0]
    zg = _dot_nt(wg_ref[...], xn) + bg_ref[...]
    kind = lax.broadcasted_iota(jnp.int32, zg.shape, 0) % GATE_ROWS
    g2 = jnp.where(kind >= 2, _log_sigmoid(zg), zg) * LOG2E
    g3 = _split_rows(g2, 3)
    u = lax.broadcasted_iota(jnp.int32, (tm, tm), 0)
    s = lax.broadcasted_iota(jnp.int32, (tm, tm), 1)
    same_chunk = (u // c) == (s // c)

    def lane_cumsum(tri):
        parts = _dot(g3, tri.astype(BF16))
        n = zg.shape[0]
        return parts[0:n] + parts[n:2 * n] + parts[2 * n:3 * n]

    prefix = lane_cumsum(same_chunk & (u <= s))
    suffix = lane_cumsum(same_chunk & (u >= s))
    rows = jnp.where(kind == 2, prefix, jnp.where(kind == 3, suffix, g2))
    for j in range(n_chunks):
        for h in range(HEADS):
            gate_ref[j, h] = rows[h * GATE_ROWS:(h + 1) * GATE_ROWS, j * c:(j + 1) * c]


def _in_proj(x, layer, norm_mix, w_main, b_main, w_gate, b_gate, lb):
    b, s, d = x.shape
    tm = min(ROW_TILE, s)
    w = GROUP_WIDTH
    grid = (b, s // tm)
    row = lambda i, j: (i, j, 0)
    fixed3 = lambda i, j: (layer, 0, 0)
    act = lambda width, dtype: jax.ShapeDtypeStruct((b, s, width), dtype)
    out_shapes = [act(w, BF16), act(w, BF16), act(w, BF16), act(w, F32), act(w, F32), act(w, BF16),
                  act(w, BF16), act(2 * w, BF16), act(w, BF16), act(w, BF16)]
    out_specs = [pl.BlockSpec((None, tm, o.shape[-1]), row) for o in out_shapes]
    out_shapes.append(jax.ShapeDtypeStruct((b, s // CHUNK, HEADS, GATE_ROWS, CHUNK), F32))
    out_specs.append(pl.BlockSpec((None, tm // CHUNK, HEADS, GATE_ROWS, CHUNK), lambda i, j: (i, j, 0, 0, 0)))
    return pl.pallas_call(
        _in_proj_kernel,
        grid=grid,
        in_specs=[
            pl.BlockSpec((None, tm, d), row),
            pl.BlockSpec((None, 1, d), fixed3),
            pl.BlockSpec((None, d, MAIN_COLS), fixed3, pipeline_mode=pl.Buffered(1)),
            pl.BlockSpec((None, 1, MAIN_COLS), fixed3),
            pl.BlockSpec((None, HEADS * GATE_ROWS, d), fixed3),
            pl.BlockSpec((None, HEADS * GATE_ROWS, 1), fixed3),
            pl.BlockSpec((None, 2, w), fixed3),
        ],
        out_specs=out_specs,
        out_shape=out_shapes,
        compiler_params=_params(("arbitrary", "arbitrary")),
        name="in_proj",
    )(x, norm_mix, w_main, b_main, w_gate, b_gate, lb)


def _hgrn_offdiag(q, k, cum, reverse):
    c = CHUNK
    r64 = lax.broadcasted_iota(jnp.int32, (c, c), 0)
    c64 = lax.broadcasted_iota(jnp.int32, (c, c), 1)
    a = None
    width = c // 2
    while width >= SUB:
        span = 2 * width
        expo = []
        for base in range(0, c, span):
            r = base + width if reverse else base + width - 1
            ref, lo, hi = cum[r:r + 1], cum[base:base + width], cum[base + width:base + span]
            expo += [lo - ref, ref - hi] if reverse else [ref - lo, hi - ref]
        e = jnp.exp2(jnp.concatenate(expo, axis=0))
        qe, ke = q * e, k * e
        zeros = jnp.zeros((width, HEAD_DIM), F32)
        qparts, kparts = [], []
        for base in range(0, c, span):
            lo, hi = slice(base, base + width), slice(base + width, base + span)
            if reverse:
                qparts += [qe[lo], zeros]
                kparts += [zeros, ke[hi]]
            else:
                qparts += [zeros, qe[hi]]
                kparts += [ke[lo], zeros]
        a_l = _dot_nt(jnp.concatenate(qparts, axis=0).astype(BF16), jnp.concatenate(kparts, axis=0).astype(BF16))
        if span < c:
            a_l = jnp.where((r64 ^ c64) < span, a_l, 0.0)
        a = a_l if a is None else a + a_l
        width //= 2
    return a


def _hgrn_diag(q, cum, c2_ref, sel_ref, mask):
    rows = []
    for b0 in range(0, CHUNK, SUB):
        cb, qb = cum[b0:b0 + SUB], q[b0:b0 + SUB]
        tiles = []
        for s in range(SUB):
            c2s = c2_ref[b0 + s:b0 + s + 1, :]
            tiles.append(qb * jnp.exp2(jnp.minimum(cb - c2s, 0.0)))
        rows.append(jnp.concatenate(tiles, axis=1))
    return _dot(jnp.concatenate(rows, axis=0).astype(BF16), sel_ref[...]) * mask


def _hgrn_diag_tables(sel_ref, mask_ref):
    k = lax.broadcasted_iota(jnp.int32, sel_ref.shape, 0)
    lane = lax.broadcasted_iota(jnp.int32, sel_ref.shape, 1)
    sel_ref[...] = (k // HEAD_DIM == lane % SUB).astype(BF16)
    t = lax.broadcasted_iota(jnp.int32, (CHUNK, HEAD_DIM), 0)
    s = lax.broadcasted_iota(jnp.int32, (CHUNK, HEAD_DIM), 1)
    same_block = (t // SUB) == (s // SUB)
    mask_ref[0] = (same_block & (s <= t)).astype(F32)
    mask_ref[1] = (same_block & (s >= t)).astype(F32)


def _finish_rows(rows, o_scr, gate_ref, gain_ref, out_ref):
    total = o_scr[rows, :]
    sq = _dot((total * total).astype(BF16), jnp.ones((HEAD_DIM, HEAD_DIM), BF16))
    yield
    y = total * lax.rsqrt(sq * (1.0 / HEAD_DIM) + NORM_EPS) * gain_ref[...] * gate_ref[rows, :].astype(F32)
    out_ref[rows, :] = y.astype(out_ref.dtype)


def _hgrn_chunk_stages(q_ref, k_ref, cum_ref, v_ref, rows, c2_ref, st_ref, o_scr, reverse, accumulate,
                       sel_ref, mask_ref):
    c = CHUNK
    q, k = q_ref[rows, :].astype(F32), k_ref[rows, :].astype(F32)
    cum, v = cum_ref[rows, :], v_ref[rows, :]
    c2_ref[...] = cum - jnp.log(k) * LOG2E
    q_in = (q * jnp.exp2(cum)).astype(BF16)
    last = cum[0:1] if reverse else cum[c - 1:c]
    kv_new = _dot_tn(v, (k * jnp.exp2(last - cum)).astype(BF16))
    a_off = _hgrn_offdiag(q, k, cum, reverse)
    diag = _hgrn_diag(q, cum, c2_ref, sel_ref, mask_ref[int(reverse)])
    yield
    o_intra = _dot((a_off + diag[:, 0:c]).astype(BF16), v)
    yield
    st = st_ref[...]
    o_inter = _dot_nt(q_in, st.astype(BF16))
    st_ref[...] = st * jnp.exp2(last) + kv_new
    yield
    if accumulate:
        o_scr[rows, :] += o_inter + o_intra
    else:
        o_scr[rows, :] = o_inter + o_intra


def _delayed(rounds, stages):
    for _ in range(rounds):
        yield
    yield from stages


def _run_interleaved(stages):
    alive = list(stages)
    while alive:
        still = []
        for g in alive:
            try:
                next(g)
                still.append(g)
            except StopIteration:
                pass
        alive = still


def _scan_loops(nc, step):
    groups = nc // UNROLL
    lax.fori_loop(0, groups // 2, lambda i, carry: (step(i * UNROLL, False), carry)[1], 0)
    lax.fori_loop(groups // 2, groups, lambda i, carry: (step(i * UNROLL, True), carry)[1], 0)


def _chunk_index(nc, first, u, reverse):
    return (nc - 1 - first - u) if reverse else first + u


def _chunk_rows(nc, first, u, reverse):
    start = _chunk_index(nc, first, u, reverse) * CHUNK
    return pl.ds(start if isinstance(start, int) else pl.multiple_of(start, CHUNK), CHUNK)


def _mlstm_chunk_stages(q_scr, k_scr, v_ref, gates_ref, ci, st_ref, n_ref, m_ref, o_scr, reverse, accumulate):
    c = CHUNK
    rows = pl.ds(pl.multiple_of(ci * c, c), c)
    ig_row, cum_row = (1, 3) if reverse else (0, 2)
    q, k = q_scr[rows, :], k_scr[rows, :]
    v_ext = jnp.concatenate([v_ref[rows, :], jnp.ones((c, HEAD_DIM), BF16)], axis=1)
    g_rows = gates_ref[ci]
    hi = g_rows.astype(BF16).astype(F32)
    mid = (g_rows - hi).astype(BF16).astype(F32)
    lo = g_rows - hi - mid
    g3 = jnp.concatenate([hi, mid, lo, jnp.zeros_like(lo)], axis=0).astype(BF16)
    part = lax.broadcasted_iota(jnp.int32, (4 * GATE_ROWS, 2 * HEAD_DIM), 0)
    want = jnp.where(lax.broadcasted_iota(jnp.int32, part.shape, 1) < HEAD_DIM, ig_row, cum_row)
    cols = _dot_tn(g3, ((part % GATE_ROWS == want) & (part < 3 * GATE_ROWS)).astype(BF16))
    ig_c, cum_c = cols[:, 0:HEAD_DIM], cols[:, HEAD_DIM:2 * HEAD_DIM]
    qk = _dot_nt(q, k)
    yield
    ig_r, cum_r = g_rows[ig_row:ig_row + 1, :], g_rows[cum_row:cum_row + 1, :]
    t_idx = lax.broadcasted_iota(jnp.int32, (c, c), 0)
    s_idx = lax.broadcasted_iota(jnp.int32, (c, c), 1)
    causal = (s_idx >= t_idx) if reverse else (s_idx <= t_idx)
    log_d = jnp.where(causal, cum_c[:, 0:c] - cum_r + ig_r, -jnp.inf)
    row_max = jnp.max(log_d, axis=-1, keepdims=True)
    last = cum_c[0:1] if reverse else cum_c[c - 1:c]
    log_w = last - cum_c + ig_c
    w_max = jnp.max(log_w, axis=0, keepdims=True)
    yield
    m = m_ref[...]
    log_inter = cum_c + m
    m_t = jnp.maximum(row_max, log_inter)
    w_inter = jnp.exp2(log_inter - m_t)
    scores = qk * jnp.exp2(log_d - m_t[:, 0:c])
    intra = _dot(scores.astype(BF16), v_ext)
    m_new = jnp.maximum(last + m, w_max)
    m_ref[...] = m_new
    carry = jnp.exp2(last + m - m_new)
    k_w = (k.astype(F32) * jnp.exp2(log_w - m_new)).astype(BF16)
    kv_new = _dot_tn(v_ext[:, 0:HEAD_DIM], k_w)
    n_new = _dot(jnp.ones((n_ref.shape[0], c), BF16), k_w)
    yield
    st, n = st_ref[...], n_ref[...]
    n_rep = jnp.broadcast_to(n[0:1, :].astype(BF16), (HEAD_DIM, HEAD_DIM))
    inter = _dot_nt(q, jnp.concatenate([st.astype(BF16), n_rep], axis=0))
    st_ref[...] = carry * st + kv_new
    n_ref[...] = carry * n + n_new
    yield
    num = intra[:, 0:HEAD_DIM] + w_inter * inter[:, 0:HEAD_DIM]
    den = intra[:, HEAD_DIM:2 * HEAD_DIM] + w_inter * inter[:, HEAD_DIM:2 * HEAD_DIM]
    h_out = num / jnp.maximum(jnp.abs(den), jnp.exp2(-m_t))
    if accumulate:
        o_scr[rows, :] += h_out
    else:
        o_scr[rows, :] = h_out


def _mlstm_conv(mq_ref, mk_ref, cwq_ref, cwk_ref, cbq_ref, cbk_ref, pad_scr, q_scr, k_scr, seq):
    c = CHUNK
    hd = HEAD_DIM
    halo = CONV_HALO
    pad_scr[0:halo, :] = jnp.zeros((halo, 2 * hd), BF16)
    pad_scr[halo + seq:2 * halo + seq, :] = jnp.zeros((halo, 2 * hd), BF16)
    pad_scr[halo:halo + seq, 0:hd] = mq_ref[...]
    pad_scr[halo:halo + seq, hd:2 * hd] = mk_ref[...]
    out_i = lax.broadcasted_iota(jnp.int32, (c, CONV_WIDTH * CONV_WIN), 0)
    col_i = lax.broadcasted_iota(jnp.int32, (c, CONV_WIDTH * CONV_WIN), 1)
    shifts = (col_i % CONV_WIN == out_i + col_i // CONV_WIN - CONV_WIDTH // 2 + halo).astype(BF16)
    taps = jnp.concatenate([cwq_ref[...], cwk_ref[...]], axis=1).astype(BF16)
    bias = jnp.concatenate([cbq_ref[...], cbk_ref[...]], axis=1)

    def conv_body(i, carry):
        start = pl.multiple_of(i * c, c)
        win = pad_scr[pl.ds(start, CONV_WIN), :]
        scaled = jnp.concatenate([win * taps[j:j + 1, :] for j in range(CONV_WIDTH)], axis=0)
        y = bias + _dot(shifts, scaled)
        y = y * _sigmoid(y)
        q_scr[pl.ds(start, c), :] = y[:, 0:hd].astype(BF16)
        k_scr[pl.ds(start, c), :] = (y[:, hd:2 * hd] * (hd ** -0.5)).astype(BF16)
        return carry

    lax.fori_loop(0, seq // c, conv_body, 0, unroll=8)


def _mixer_kernel(hq_ref, hkf_ref, hkb_ref, hcf_ref, hcb_ref, hv_ref, hg_ref, hgain_ref,
                  mq_ref, mk_ref, cwq_ref, cwk_ref, cbq_ref, cbk_ref, mv_ref, mo_ref, gates_ref, mgain_ref,
                  hout_ref, mout_ref,
                  ho_scr, c2_scr, hstf_scr, hstb_scr, sel_scr, mask_scr,
                  pad_scr, q_scr, k_scr, mo_scr, mstf_scr, mstb_scr, mnf_scr, mnb_scr, mf_scr, mb_scr, *, seq):
    nc = seq // CHUNK
    _mlstm_conv(mq_ref, mk_ref, cwq_ref, cwk_ref, cbq_ref, cbk_ref, pad_scr, q_scr, k_scr, seq)
    for ref in (hstf_scr, hstb_scr, mstf_scr, mstb_scr, mnf_scr, mnb_scr, mf_scr, mb_scr):
        ref[...] = jnp.zeros_like(ref)
    _hgrn_diag_tables(sel_scr, mask_scr)

    def finish_stages(first):
        stages = []
        for u in range(UNROLL):
            for reverse in (False, True):
                rows = _chunk_rows(nc, first, u, reverse)
                stages.append(_finish_rows(rows, ho_scr, hg_ref, hgain_ref, hout_ref))
                stages.append(_finish_rows(rows, mo_scr, mo_ref, mgain_ref, mout_ref))
        return stages

    def step(first, accumulate):
        stages = finish_stages(first - UNROLL) if accumulate else []
        for u in range(UNROLL):
            for reverse in (False, True):
                stages.append(_delayed(u * STAGE_SKEW, _hgrn_chunk_stages(
                    hq_ref, hkb_ref if reverse else hkf_ref, hcb_ref if reverse else hcf_ref, hv_ref,
                    _chunk_rows(nc, first, u, reverse), c2_scr.at[2 * u + int(reverse)],
                    hstb_scr if reverse else hstf_scr, ho_scr, reverse, accumulate, sel_scr, mask_scr)))
                stages.append(_delayed(u * STAGE_SKEW, _mlstm_chunk_stages(
                    q_scr, k_scr, mv_ref, gates_ref, _chunk_index(nc, first, u, reverse),
                    mstb_scr if reverse else mstf_scr, mnb_scr if reverse else mnf_scr,
                    mb_scr if reverse else mf_scr, mo_scr, reverse, accumulate)))
        _run_interleaved(stages)

    _scan_loops(nc, step)
    _run_interleaved(finish_stages(nc - UNROLL))


def _mixer_scan(hq, hkf, hkb, hcf, hcb, hv, hg, hgain, mqk, conv_w, conv_b, mv, mo, gates, mgain, layer):
    b, s, _ = hq.shape
    hd = HEAD_DIM
    blk = lambda off=0: pl.BlockSpec((None, s, hd), lambda i, h: (i, 0, h + off))
    par = lambda rows, off=0: pl.BlockSpec((None, rows, hd), lambda i, h: (layer, 0, h + off))
    out = jax.ShapeDtypeStruct((b, s, GROUP_WIDTH), BF16)
    return pl.pallas_call(
        functools.partial(_mixer_kernel, seq=s),
        grid=(b, HEADS),
        in_specs=[blk(), blk(), blk(), blk(), blk(), blk(), blk(), par(1),
                  blk(), blk(HEADS), par(CONV_WIDTH), par(CONV_WIDTH, HEADS), par(1), par(1, HEADS),
                  blk(), blk(),
                  pl.BlockSpec((None, s // CHUNK, None, GATE_ROWS, CHUNK), lambda i, h: (i, 0, h, 0, 0)),
                  par(1)],
        out_specs=[blk(), blk()],
        out_shape=[out, out],
        scratch_shapes=[pltpu.VMEM((s, hd), F32), pltpu.VMEM((2 * UNROLL, CHUNK, hd), F32),
                        pltpu.VMEM((hd, hd), F32), pltpu.VMEM((hd, hd), F32),
                        pltpu.VMEM((SUB * hd, hd), BF16), pltpu.VMEM((2, CHUNK, hd), F32),
                        pltpu.VMEM((s + 2 * CONV_HALO, 2 * hd), BF16),
                        pltpu.VMEM((s, hd), BF16), pltpu.VMEM((s, hd), BF16), pltpu.VMEM((s, hd), F32),
                        pltpu.VMEM((hd, hd), F32), pltpu.VMEM((hd, hd), F32),
                        pltpu.VMEM((8, hd), F32), pltpu.VMEM((8, hd), F32),
                        pltpu.VMEM((1, hd), F32), pltpu.VMEM((1, hd), F32)],
        compiler_params=_params(("arbitrary", "arbitrary")),
        name="mixer_scan",
    )(hq, hkf, hkb, hcf, hcb, hv, hg, hgain, mqk, mqk, conv_w, conv_w, conv_b, conv_b, mv, mo, gates, mgain)


def _attn_kernel(x_ref, hmix_ref, mmix_ref, wout_ref, g_ref, wq_ref, kv_ref, wo_ref, out_ref):
    w = GROUP_WIDTH
    hd = XATTN_HEAD_DIM
    x1 = x_ref[...] + _dot(hmix_ref[...], wout_ref[0:w, :]) + _dot(mmix_ref[...], wout_ref[w:2 * w, :])
    xn = _rms_scale(x1, g_ref[...]).astype(BF16)
    q = (_dot(xn, wq_ref[...]) * (hd ** -0.5)).astype(BF16)
    heads = []
    for h in range(HEADS):
        kh = kv_ref[:, h * hd:(h + 1) * hd]
        vh = kv_ref[:, D_MODEL + h * hd:D_MODEL + (h + 1) * hd]
        sc = _dot_nt(q[:, h * hd:(h + 1) * hd], kh)
        p = jnp.exp(sc - jnp.max(sc, axis=-1, keepdims=True))
        o = _dot(p.astype(BF16), vh) / jnp.sum(p, axis=-1, keepdims=True)
        heads.append(o.astype(BF16))
    out_ref[...] = x1 + _dot(jnp.concatenate(heads, axis=1), wo_ref[...])


def _attn_block(x, hmix, mmix, w_out, norm_xattn, w_xq, kv, w_xo, layer):
    b, s, d = x.shape
    tm = min(WIDE_ROW_TILE, s)
    m = kv.shape[2]
    row = lambda i, j: (i, j, 0)
    fixed3 = lambda i, j: (layer, 0, 0)
    wspec = lambda: pl.BlockSpec((None, d, d), fixed3, pipeline_mode=pl.Buffered(1))
    return pl.pallas_call(
        _attn_kernel,
        grid=(b, s // tm),
        in_specs=[
            pl.BlockSpec((None, tm, d), row),
            pl.BlockSpec((None, tm, GROUP_WIDTH), row),
            pl.BlockSpec((None, tm, GROUP_WIDTH), row),
            wspec(),
            pl.BlockSpec((None, 1, d), fixed3),
            wspec(),
            pl.BlockSpec((None, None, m, 2 * d), lambda i, j: (layer, i, 0, 0)),
            wspec(),
        ],
        out_specs=pl.BlockSpec((None, tm, d), row),
        out_shape=jax.ShapeDtypeStruct((b, s, d), F32),
        compiler_params=_params(("arbitrary", "arbitrary")),
        name="attn_block",
    )(x, hmix, mmix, w_out, norm_xattn, w_xq, kv, w_xo)


def _mlp_kernel(x_ref, g_ref, wup_ref, wdn_ref, gfin_ref, out_ref, *, final):
    x = x_ref[...]
    xn = _rms_scale(x, g_ref[...]).astype(BF16)
    acc = x
    step = D_MODEL
    for c0 in range(0, D_FF, step):
        hcol = jnp.maximum(_dot(xn, wup_ref[:, c0:c0 + step]), 0.0)
        acc = acc + _dot((hcol * hcol).astype(BF16), wdn_ref[c0:c0 + step, :])
    if final:
        acc = _rms_scale(acc, gfin_ref[...])
    out_ref[...] = acc


def _mlp_block(x, norm_mlp, w_up, w_down, norm_final, layer, final):
    b, s, d = x.shape
    tm = min(WIDE_ROW_TILE, s)
    row = lambda i, j: (i, j, 0)
    fixed3 = lambda i, j: (layer, 0, 0)
    return pl.pallas_call(
        functools.partial(_mlp_kernel, final=final),
        grid=(b, s // tm),
        in_specs=[
            pl.BlockSpec((None, tm, d), row),
            pl.BlockSpec((None, 1, d), fixed3),
            pl.BlockSpec((None, d, D_FF), fixed3, pipeline_mode=pl.Buffered(1)),
            pl.BlockSpec((None, D_FF, d), fixed3, pipeline_mode=pl.Buffered(1)),
            pl.BlockSpec((1, d), lambda i, j: (0, 0)),
        ],
        out_specs=pl.BlockSpec((None, tm, d), row),
        out_shape=jax.ShapeDtypeStruct((b, s, d), F32),
        compiler_params=_params(("arbitrary", "arbitrary")),
        name="mlp_block",
    )(x, norm_mlp, w_up, w_down, norm_final)


def kernel(x, mem, norm_mix, norm_xattn, norm_mem, norm_mlp, norm_final, w_in, b_in, hgrn_lb_logits, hgrn_norm,
           mlstm_conv_w, mlstm_conv_b, mlstm_norm, w_out, w_xq, w_xkv, w_xo, w_up, w_down):
    depth = w_in.shape[0]
    b, s, d = x.shape
    assert d == D_MODEL and s % (2 * UNROLL * CHUNK) == 0 and s % min(WIDE_ROW_TILE, s) == 0
    assert w_in.shape[-1] == MAIN_COLS + 4 * HEADS

    row3 = lambda t: t.reshape(depth, 1, t.shape[-1])
    w_gate = w_in[:, :, MAIN_COLS:].reshape(depth, d, 4, HEADS).transpose(0, 3, 2, 1)
    w_gate = jnp.pad(w_gate, ((0, 0), (0, 0), (0, GATE_ROWS - 4), (0, 0))).reshape(depth, HEADS * GATE_ROWS, d)
    b_gate = b_in[:, MAIN_COLS:].reshape(depth, 4, HEADS).transpose(0, 2, 1)
    b_gate = jnp.pad(b_gate, ((0, 0), (0, 0), (0, GATE_ROWS - 4))).reshape(depth, HEADS * GATE_ROWS, 1)
    w_main = w_in.astype(BF16)
    b_main = row3(b_in)
    w_gate = w_gate.astype(BF16)
    w_out_b, w_xq_b, w_xkv_b, w_xo_b = (t.astype(BF16) for t in (w_out, w_xq, w_xkv, w_xo))
    w_up_b, w_down_b = w_up.astype(BF16), w_down.astype(BF16)
    norm_mix3, norm_xattn3, norm_mlp3 = row3(norm_mix), row3(norm_xattn), row3(norm_mlp)
    hgrn_norm3, mlstm_norm3, conv_b3 = row3(hgrn_norm), row3(mlstm_norm), row3(mlstm_conv_b)
    norm_final2 = norm_final.reshape(1, d)

    lb = _lower_bounds(hgrn_lb_logits)
    kv = _mem_kv(mem, norm_mem, w_xkv_b)

    for l in range(depth):
        (hq, hkf, hkb, hcf, hcb, hv, hg, mqk, mv, mo, gates) = _in_proj(
            x, l, norm_mix3, w_main, b_main, w_gate, b_gate, lb)
        hmix, mmix = _mixer_scan(hq, hkf, hkb, hcf, hcb, hv, hg, hgrn_norm3,
                                 mqk, mlstm_conv_w, conv_b3, mv, mo, gates, mlstm_norm3, l)
        x = _attn_block(x, hmix, mmix, w_out_b, norm_xattn3, w_xq_b, kv, w_xo_b, l)
        x = _mlp_block(x, norm_mlp3, w_up_b, w_down_b, norm_final2, l, final=(l == depth - 1))
    return x
```

```python
import functools

import jax
import jax.numpy as jnp
from jax import lax
from jax.experimental import pallas as pl
from jax.experimental.pallas import tpu as pltpu

F32 = jnp.float32
BF16 = jnp.bfloat16

D_MODEL = 1024
HEADS = 4
HEAD_DIM = 128
GROUP_WIDTH = HEADS * HEAD_DIM
CONV_WIDTH = 5
XATTN_HEAD_DIM = D_MODEL // HEADS
D_FF = 4 * D_MODEL
CHUNK = 64
SUB = 8
DECAY_TERMS = 3
GATE_ROWS = 8
UNROLL = 8
STAGE_SKEW = 1
NORM_EPS = 1e-6
LOG2E = 1.4426950408889634
MAIN_COLS = 9 * GROUP_WIDTH
CONV_HALO = 16
CONV_WIN = CHUNK + 2 * CONV_HALO
ROW_TILE = 512
WIDE_ROW_TILE = 1024
VMEM_LIMIT = 56 * 1024 * 1024


def _dot(a, b):
    return jnp.dot(a, b, preferred_element_type=F32)


def _dot_nt(a, b):
    return lax.dot_general(a, b, (((1,), (1,)), ((), ())), preferred_element_type=F32)


def _dot_tn(a, b):
    return lax.dot_general(a, b, (((0,), (0,)), ((), ())), preferred_element_type=F32)


def _sigmoid(z):
    return 1.0 / (1.0 + jnp.exp(-z))


def _log_sigmoid(z):
    return jnp.minimum(z, 0.0) - jnp.log(1.0 + jnp.exp(-jnp.abs(z)))


def _rms_scale(x, gain):
    return x * lax.rsqrt(jnp.mean(x * x, axis=-1, keepdims=True) + NORM_EPS) * gain


def _split_rows(x, terms):
    parts, rest = [], x
    for i in range(terms):
        part = rest.astype(BF16)
        parts.append(part)
        if i + 1 < terms:
            rest = rest - part.astype(F32)
    return jnp.concatenate(parts, axis=0)


def _tri(reverse, terms):
    r = lax.broadcasted_iota(jnp.int32, (CHUNK, terms * CHUNK), 0)
    s = lax.broadcasted_iota(jnp.int32, (CHUNK, terms * CHUNK), 1) % CHUNK
    return ((s >= r) if reverse else (s <= r)).astype(BF16)


def _params(semantics):
    return pltpu.CompilerParams(dimension_semantics=semantics, vmem_limit_bytes=VMEM_LIMIT)


def _lb_kernel(logit_ref, lb_ref, *, depth):
    for d in range(2):
        rows = [logit_ref[d * depth + l:d * depth + l + 1, :] for l in range(depth)]
        mx = functools.reduce(jnp.maximum, rows)
        es = [jnp.exp(r - mx) for r in rows]
        tot = functools.reduce(lambda a, b: a + b, es)
        c = None
        c0 = None
        for l in range(depth):
            p = es[l] / tot
            c = p if c is None else c + p
            if l == 0:
                c0 = c
            lb_ref[2 * l + d:2 * l + d + 1, :] = c - c0


def _lower_bounds(logits):
    _, depth, width = logits.shape
    out = pl.pallas_call(
        functools.partial(_lb_kernel, depth=depth),
        out_shape=jax.ShapeDtypeStruct((2 * depth, width), F32),
        name="lower_bounds",
    )(logits.astype(F32).reshape(2 * depth, width))
    return out.reshape(depth, 2, width)


def _mem_kv_kernel(mem_ref, g_ref, w_ref, kv_ref):
    memn = _rms_scale(mem_ref[...], g_ref[...]).astype(BF16)
    kv_ref[...] = _dot(memn, w_ref[...]).astype(BF16)


def _mem_kv(mem, norm_mem, w_xkv):
    b, m, d = mem.shape
    depth = norm_mem.shape[0]
    return pl.pallas_call(
        _mem_kv_kernel,
        grid=(depth, b),
        in_specs=[
            pl.BlockSpec((None, m, d), lambda l, i: (i, 0, 0)),
            pl.BlockSpec((None, 1, d), lambda l, i: (l, 0, 0)),
            pl.BlockSpec((None, d, 2 * d), lambda l, i: (l, 0, 0)),
        ],
        out_specs=pl.BlockSpec((None, None, m, 2 * d), lambda l, i: (l, i, 0, 0)),
        out_shape=jax.ShapeDtypeStruct((depth, b, m, 2 * d), BF16),
        compiler_params=_params(("arbitrary", "arbitrary")),
        name="mem_kv",
    )(mem, norm_mem.reshape(depth, 1, d), w_xkv)


def _in_proj_kernel(x_ref, g_ref, w_ref, b_ref, wg_ref, bg_ref, lb_ref,
                    hq_ref, hkf_ref, hkb_ref, hcf_ref, hcb_ref, hv_ref, hg_ref,
                    mqk_ref, mv_ref, mo_ref, gate_ref):
    w = GROUP_WIDTH
    c = CHUNK
    n_chunks = x_ref.shape[0] // c
    xn = _rms_scale(x_ref[...], g_ref[...]).astype(BF16)
    tri_f, tri_b = _tri(False, DECAY_TERMS), _tri(True, DECAY_TERMS)

    def proj(group):
        c0 = group * w
        return _dot(xn, w_ref[:, c0:c0 + w]) + b_ref[:, c0:c0 + w]

    def silu_to(z, ref):
        ref[...] = (z * _sigmoid(z)).astype(BF16)

    def forget_rows(z, lb, tri, k_ref, cum_ref, r0, r1):
        z = z[r0:r1]
        e = jnp.exp(-jnp.abs(z))
        log_sig = jnp.minimum(z, 0.0) - jnp.log(1.0 + e)
        a = jnp.log(lb)
        t = jnp.log1p(-lb) + log_sig
        lf2 = (jnp.maximum(a, t) + jnp.log(1.0 + jnp.exp(-jnp.abs(a - t)))) * LOG2E
        k_ref[r0:r1, :] = ((1.0 - lb) * (jnp.where(z >= 0, e, 1.0) / (1.0 + e))).astype(BF16)
        for j in range((r1 - r0) // c):
            cum_ref[r0 + j * c:r0 + (j + 1) * c, :] = _dot(tri, _split_rows(lf2[j * c:(j + 1) * c], DECAY_TERMS))

    half = x_ref.shape[0] // 2
    lb_f, lb_b = lb_ref[0:1, :], lb_ref[1:2, :]
    z_q, z_ff = proj(0), proj(1)
    z_fb = proj(2)
    silu_to(z_q, hq_ref)
    z_v = proj(3)
    forget_rows(z_ff, lb_f, tri_f, hkf_ref, hcf_ref, 0, half)
    z_g = proj(4)
    forget_rows(z_ff, lb_f, tri_f, hkf_ref, hcf_ref, half, 2 * half)
    hv_ref[...] = z_v.astype(BF16)
    z_mq = proj(5)
    forget_rows(z_fb, lb_b, tri_b, hkb_ref, hcb_ref, 0, half)
    z_mk = proj(6)
    forget_rows(z_fb, lb_b, tri_b, hkb_ref, hcb_ref, half, 2 * half)
    mqk_ref[:, 0:w] = z_mq.astype(BF16)
    z_mv = proj(7)
    silu_to(z_g, hg_ref)
    mqk_ref[:, w:2 * w] = z_mk.astype(BF16)
    z_mo = proj(8)
    mv_ref[...] = z_mv.astype(BF16)
    mo_ref[...] = _sigmoid(z_mo).astype(BF16)
    tm = x_ref.shape[0]
    zg = _dot_nt(wg_ref[...], xn) + bg_ref[...]
    kind = lax.broadcasted_iota(jnp.int32, zg.shape, 0) % GATE_ROWS
    g2 = jnp.where(kind >= 2, _log_sigmoid(zg), zg) * LOG2E
    g3 = _split_rows(g2, 3)
    u = lax.broadcasted_iota(jnp.int32, (tm, tm), 0)
    s = lax.broadcasted_iota(jnp.int32, (tm, tm), 1)
    same_chunk = (u // c) == (s // c)

    def lane_cumsum(tri):
        parts = _dot(g3, tri.astype(BF16))
        n = zg.shape[0]
        return parts[0:n] + parts[n:2 * n] + parts[2 * n:3 * n]

    prefix = lane_cumsum(same_chunk & (u <= s))
    suffix = lane_cumsum(same_chunk & (u >= s))
    rows = jnp.where(kind == 2, prefix, jnp.where(kind == 3, suffix, g2))
    for j in range(n_chunks):
        for h in range(HEADS):
            gate_ref[j, h] = rows[h * GATE_ROWS:(h + 1) * GATE_ROWS, j * c:(j + 1) * c]


def _in_proj(x, layer, norm_mix, w_main, b_main, w_gate, b_gate, lb):
    b, s, d = x.shape
    tm = min(ROW_TILE, s)
    w = GROUP_WIDTH
    grid = (b, s // tm)
    row = lambda i, j: (i, j, 0)
    fixed3 = lambda i, j: (layer, 0, 0)
    act = lambda width, dtype: jax.ShapeDtypeStruct((b, s, width), dtype)
    out_shapes = [act(w, BF16), act(w, BF16), act(w, BF16), act(w, F32), act(w, F32), act(w, BF16),
                  act(w, BF16), act(2 * w, BF16), act(w, BF16), act(w, BF16)]
    out_specs = [pl.BlockSpec((None, tm, o.shape[-1]), row) for o in out_shapes]
    out_shapes.append(jax.ShapeDtypeStruct((b, s // CHUNK, HEADS, GATE_ROWS, CHUNK), F32))
    out_specs.append(pl.BlockSpec((None, tm // CHUNK, HEADS, GATE_ROWS, CHUNK), lambda i, j: (i, j, 0, 0, 0)))
    return pl.pallas_call(
        _in_proj_kernel,
        grid=grid,
        in_specs=[
            pl.BlockSpec((None, tm, d), row),
            pl.BlockSpec((None, 1, d), fixed3),
            pl.BlockSpec((None, d, MAIN_COLS), fixed3, pipeline_mode=pl.Buffered(1)),
            pl.BlockSpec((None, 1, MAIN_COLS), fixed3),
            pl.BlockSpec((None, HEADS * GATE_ROWS, d), fixed3),
            pl.BlockSpec((None, HEADS * GATE_ROWS, 1), fixed3),
            pl.BlockSpec((None, 2, w), fixed3),
        ],
        out_specs=out_specs,
        out_shape=out_shapes,
        compiler_params=_params(("arbitrary", "arbitrary")),
        name="in_proj",
    )(x, norm_mix, w_main, b_main, w_gate, b_gate, lb)


def _hgrn_offdiag(q, k, cum, reverse):
    c = CHUNK
    r64 = lax.broadcasted_iota(jnp.int32, (c, c), 0)
    c64 = lax.broadcasted_iota(jnp.int32, (c, c), 1)
    a = None
    width = c // 2
    while width >= SUB:
        span = 2 * width
        expo = []
        for base in range(0, c, span):
            r = base + width if reverse else base + width - 1
            ref, lo, hi = cum[r:r + 1], cum[base:base + width], cum[base + width:base + span]
            expo += [lo - ref, ref - hi] if reverse else [ref - lo, hi - ref]
        e = jnp.exp2(jnp.concatenate(expo, axis=0))
        qe, ke = q * e, k * e
        zeros = jnp.zeros((width, HEAD_DIM), F32)
        qparts, kparts = [], []
        for base in range(0, c, span):
            lo, hi = slice(base, base + width), slice(base + width, base + span)
            if reverse:
                qparts += [qe[lo], zeros]
                kparts += [zeros, ke[hi]]
            else:
                qparts += [zeros, qe[hi]]
                kparts += [ke[lo], zeros]
        a_l = _dot_nt(jnp.concatenate(qparts, axis=0).astype(BF16), jnp.concatenate(kparts, axis=0).astype(BF16))
        if span < c:
            a_l = jnp.where((r64 ^ c64) < span, a_l, 0.0)
        a = a_l if a is None else a + a_l
        width //= 2
    return a


def _hgrn_diag(q, cum, c2_ref, sel_ref, mask):
    rows = []
    for b0 in range(0, CHUNK, SUB):
        cb, qb = cum[b0:b0 + SUB], q[b0:b0 + SUB]
        tiles = []
        for s in range(SUB):
            c2s = c2_ref[b0 + s:b0 + s + 1, :]
            tiles.append(qb * jnp.exp2(jnp.minimum(cb - c2s, 0.0)))
        rows.append(jnp.concatenate(tiles, axis=1))
    return _dot(jnp.concatenate(rows, axis=0).astype(BF16), sel_ref[...]) * mask


def _hgrn_diag_tables(sel_ref, mask_ref):
    k = lax.broadcasted_iota(jnp.int32, sel_ref.shape, 0)
    lane = lax.broadcasted_iota(jnp.int32, sel_ref.shape, 1)
    sel_ref[...] = (k // HEAD_DIM == lane % SUB).astype(BF16)
    t = lax.broadcasted_iota(jnp.int32, (CHUNK, HEAD_DIM), 0)
    s = lax.broadcasted_iota(jnp.int32, (CHUNK, HEAD_DIM), 1)
    same_block = (t // SUB) == (s // SUB)
    mask_ref[0] = (same_block & (s <= t)).astype(F32)
    mask_ref[1] = (same_block & (s >= t)).astype(F32)


def _finish_rows(rows, o_scr, gate_ref, gain_ref, out_ref):
    total = o_scr[rows, :]
    sq = _dot((total * total).astype(BF16), jnp.ones((HEAD_DIM, HEAD_DIM), BF16))
    yield
    y = total * lax.rsqrt(sq * (1.0 / HEAD_DIM) + NORM_EPS) * gain_ref[...] * gate_ref[rows, :].astype(F32)
    out_ref[rows, :] = y.astype(out_ref.dtype)


def _hgrn_chunk_stages(q_ref, k_ref, cum_ref, v_ref, rows, c2_ref, st_ref, o_scr, reverse, accumulate,
                       sel_ref, mask_ref):
    c = CHUNK
    q, k = q_ref[rows, :].astype(F32), k_ref[rows, :].astype(F32)
    cum, v = cum_ref[rows, :], v_ref[rows, :]
    c2_ref[...] = cum - jnp.log(k) * LOG2E
    q_in = (q * jnp.exp2(cum)).astype(BF16)
    last = cum[0:1] if reverse else cum[c - 1:c]
    kv_new = _dot_tn(v, (k * jnp.exp2(last - cum)).astype(BF16))
    a_off = _hgrn_offdiag(q, k, cum, reverse)
    diag = _hgrn_diag(q, cum, c2_ref, sel_ref, mask_ref[int(reverse)])
    yield
    o_intra = _dot((a_off + diag[:, 0:c]).astype(BF16), v)
    yield
    st = st_ref[...]
    o_inter = _dot_nt(q_in, st.astype(BF16))
    st_ref[...] = st * jnp.exp2(last) + kv_new
    yield
    if accumulate:
        o_scr[rows, :] += o_inter + o_intra
    else:
        o_scr[rows, :] = o_inter + o_intra


def _delayed(rounds, stages):
    for _ in range(rounds):
        yield
    yield from stages


def _run_interleaved(stages):
    alive = list(stages)
    while alive:
        still = []
        for g in alive:
            try:
                next(g)
                still.append(g)
            except StopIteration:
                pass
        alive = still


def _scan_loops(nc, step):
    groups = nc // UNROLL
    half = groups // 2
    lax.fori_loop(0, half, lambda i, carry: (step(i * UNROLL, False, False), carry)[1], 0)
    step(half * UNROLL, True, False)
    lax.fori_loop(half + 1, groups, lambda i, carry: (step(i * UNROLL, True, True), carry)[1], 0)


def _chunk_index(nc, first, u, reverse):
    return (nc - 1 - first - u) if reverse else first + u


def _rows_of_chunk(ci):
    start = ci * CHUNK
    return pl.ds(start if isinstance(start, int) else pl.multiple_of(start, CHUNK), CHUNK)


def _chunk_rows(nc, first, u, reverse):
    return _rows_of_chunk(_chunk_index(nc, first, u, reverse))


def _mlstm_chunk_stages(q_scr, k_scr, v_ref, gates_ref, ci, st_ref, n_ref, m_ref, o_scr, reverse, accumulate):
    c = CHUNK
    rows = _rows_of_chunk(ci)
    ig_row, cum_row = (1, 3) if reverse else (0, 2)
    q, k = q_scr[rows, :], k_scr[rows, :]
    v_ext = jnp.concatenate([v_ref[rows, :], jnp.ones((c, HEAD_DIM), BF16)], axis=1)
    g_rows = gates_ref[ci]
    hi = g_rows.astype(BF16).astype(F32)
    mid = (g_rows - hi).astype(BF16).astype(F32)
    lo = g_rows - hi - mid
    g3 = jnp.concatenate([hi, mid, lo, jnp.zeros_like(lo)], axis=0).astype(BF16)
    part = lax.broadcasted_iota(jnp.int32, (4 * GATE_ROWS, 2 * HEAD_DIM), 0)
    want = jnp.where(lax.broadcasted_iota(jnp.int32, part.shape, 1) < HEAD_DIM, ig_row, cum_row)
    cols = _dot_tn(g3, ((part % GATE_ROWS == want) & (part < 3 * GATE_ROWS)).astype(BF16))
    ig_c, cum_c = cols[:, 0:HEAD_DIM], cols[:, HEAD_DIM:2 * HEAD_DIM]
    qk = _dot_nt(q, k)
    yield
    ig_r, cum_r = g_rows[ig_row:ig_row + 1, :], g_rows[cum_row:cum_row + 1, :]
    t_idx = lax.broadcasted_iota(jnp.int32, (c, c), 0)
    s_idx = lax.broadcasted_iota(jnp.int32, (c, c), 1)
    causal = (s_idx >= t_idx) if reverse else (s_idx <= t_idx)
    log_d = jnp.where(causal, cum_c[:, 0:c] - cum_r + ig_r, -jnp.inf)
    row_max = jnp.max(log_d, axis=-1, keepdims=True)
    last = cum_c[0:1] if reverse else cum_c[c - 1:c]
    log_w = last - cum_c + ig_c
    w_max = jnp.max(log_w, axis=0, keepdims=True)
    yield
    m = m_ref[...]
    log_inter = cum_c + m
    m_t = jnp.maximum(row_max, log_inter)
    w_inter = jnp.exp2(log_inter - m_t)
    scores = qk * jnp.exp2(log_d - m_t[:, 0:c])
    intra = _dot(scores.astype(BF16), v_ext)
    m_new = jnp.maximum(last + m, w_max)
    m_ref[...] = m_new
    carry = jnp.exp2(last + m - m_new)
    k_w = (k.astype(F32) * jnp.exp2(log_w - m_new)).astype(BF16)
    kv_new = _dot_tn(v_ext[:, 0:HEAD_DIM], k_w)
    n_new = _dot(jnp.ones((n_ref.shape[0], c), BF16), k_w)
    yield
    st, n = st_ref[...], n_ref[...]
    n_rep = jnp.broadcast_to(n[0:1, :].astype(BF16), (HEAD_DIM, HEAD_DIM))
    inter = _dot_nt(q, jnp.concatenate([st.astype(BF16), n_rep], axis=0))
    st_ref[...] = carry * st + kv_new
    n_ref[...] = carry * n + n_new
    yield
    num = intra[:, 0:HEAD_DIM] + w_inter * inter[:, 0:HEAD_DIM]
    den = intra[:, HEAD_DIM:2 * HEAD_DIM] + w_inter * inter[:, HEAD_DIM:2 * HEAD_DIM]
    h_out = num / jnp.maximum(jnp.abs(den), jnp.exp2(-m_t))
    if accumulate:
        o_scr[rows, :] += h_out
    else:
        o_scr[rows, :] = h_out


def _mlstm_conv(mq_ref, mk_ref, cwq_ref, cwk_ref, cbq_ref, cbk_ref, pad_scr, q_scr, k_scr, seq):
    c = CHUNK
    hd = HEAD_DIM
    halo = CONV_HALO
    pad_scr[0:halo, :] = jnp.zeros((halo, 2 * hd), BF16)
    pad_scr[halo + seq:2 * halo + seq, :] = jnp.zeros((halo, 2 * hd), BF16)
    pad_scr[halo:halo + seq, 0:hd] = mq_ref[...]
    pad_scr[halo:halo + seq, hd:2 * hd] = mk_ref[...]
    out_i = lax.broadcasted_iota(jnp.int32, (c, CONV_WIDTH * CONV_WIN), 0)
    col_i = lax.broadcasted_iota(jnp.int32, (c, CONV_WIDTH * CONV_WIN), 1)
    shifts = (col_i % CONV_WIN == out_i + col_i // CONV_WIN - CONV_WIDTH // 2 + halo).astype(BF16)
    taps = jnp.concatenate([cwq_ref[...], cwk_ref[...]], axis=1).astype(BF16)
    bias = jnp.concatenate([cbq_ref[...], cbk_ref[...]], axis=1)

    def conv_body(i, carry):
        start = pl.multiple_of(i * c, c)
        win = pad_scr[pl.ds(start, CONV_WIN), :]
        scaled = jnp.concatenate([win * taps[j:j + 1, :] for j in range(CONV_WIDTH)], axis=0)
        y = bias + _dot(shifts, scaled)
        y = y * _sigmoid(y)
        q_scr[pl.ds(start, c), :] = y[:, 0:hd].astype(BF16)
        k_scr[pl.ds(start, c), :] = (y[:, hd:2 * hd] * (hd ** -0.5)).astype(BF16)
        return carry

    lax.fori_loop(0, seq // c, conv_body, 0, unroll=8)


def _mixer_kernel(hq_ref, hkf_ref, hkb_ref, hcf_ref, hcb_ref, hv_ref, hg_ref, hgain_ref,
                  mq_ref, mk_ref, cwq_ref, cwk_ref, cbq_ref, cbk_ref, mv_ref, mo_ref, gates_ref, mgain_ref,
                  hout_ref, mout_ref,
                  ho_scr, c2_scr, hstf_scr, hstb_scr, sel_scr, mask_scr,
                  pad_scr, q_scr, k_scr, mo_scr, mstf_scr, mstb_scr, mnf_scr, mnb_scr, mf_scr, mb_scr, *, seq):
    nc = seq // CHUNK
    _mlstm_conv(mq_ref, mk_ref, cwq_ref, cwk_ref, cbq_ref, cbk_ref, pad_scr, q_scr, k_scr, seq)
    for ref in (hstf_scr, hstb_scr, mstf_scr, mstb_scr, mnf_scr, mnb_scr, mf_scr, mb_scr):
        ref[...] = jnp.zeros_like(ref)
    _hgrn_diag_tables(sel_scr, mask_scr)

    def finish_stages(first):
        stages = []
        for u in range(UNROLL):
            for reverse in (False, True):
                rows = _chunk_rows(nc, first, u, reverse)
                stages.append(_finish_rows(rows, ho_scr, hg_ref, hgain_ref, hout_ref))
                stages.append(_finish_rows(rows, mo_scr, mo_ref, mgain_ref, mout_ref))
        return stages

    def step(first, accumulate, finish_previous):
        stages = finish_stages(first - UNROLL) if finish_previous else []
        for u in range(UNROLL):
            for reverse in (False, True):
                stages.append(_delayed(u * STAGE_SKEW, _hgrn_chunk_stages(
                    hq_ref, hkb_ref if reverse else hkf_ref, hcb_ref if reverse else hcf_ref, hv_ref,
                    _chunk_rows(nc, first, u, reverse), c2_scr.at[2 * u + int(reverse)],
                    hstb_scr if reverse else hstf_scr, ho_scr, reverse, accumulate, sel_scr, mask_scr)))
                stages.append(_delayed(u * STAGE_SKEW, _mlstm_chunk_stages(
                    q_scr, k_scr, mv_ref, gates_ref, _chunk_index(nc, first, u, reverse),
                    mstb_scr if reverse else mstf_scr, mnb_scr if reverse else mnf_scr,
                    mb_scr if reverse else mf_scr, mo_scr, reverse, accumulate)))
        _run_interleaved(stages)

    _scan_loops(nc, step)
    _run_interleaved(finish_stages(nc - UNROLL))


def _mixer_scan(hq, hkf, hkb, hcf, hcb, hv, hg, hgain, mqk, conv_w, conv_b, mv, mo, gates, mgain, layer):
    b, s, _ = hq.shape
    hd = HEAD_DIM
    blk = lambda off=0: pl.BlockSpec((None, s, hd), lambda i, h: (i, 0, h + off))
    par = lambda rows, off=0: pl.BlockSpec((None, rows, hd), lambda i, h: (layer, 0, h + off))
    out = jax.ShapeDtypeStruct((b, s, GROUP_WIDTH), BF16)
    return pl.pallas_call(
        functools.partial(_mixer_kernel, seq=s),
        grid=(b, HEADS),
        in_specs=[blk(), blk(), blk(), blk(), blk(), blk(), blk(), par(1),
                  blk(), blk(HEADS), par(CONV_WIDTH), par(CONV_WIDTH, HEADS), par(1), par(1, HEADS),
                  blk(), blk(),
                  pl.BlockSpec((None, s // CHUNK, None, GATE_ROWS, CHUNK), lambda i, h: (i, 0, h, 0, 0)),
                  par(1)],
        out_specs=[blk(), blk()],
        out_shape=[out, out],
        scratch_shapes=[pltpu.VMEM((s, hd), F32), pltpu.VMEM((2 * UNROLL, CHUNK, hd), F32),
                        pltpu.VMEM((hd, hd), F32), pltpu.VMEM((hd, hd), F32),
                        pltpu.VMEM((SUB * hd, hd), BF16), pltpu.VMEM((2, CHUNK, hd), F32),
                        pltpu.VMEM((s + 2 * CONV_HALO, 2 * hd), BF16),
                        pltpu.VMEM((s, hd), BF16), pltpu.VMEM((s, hd), BF16), pltpu.VMEM((s, hd), F32),
                        pltpu.VMEM((hd, hd), F32), pltpu.VMEM((hd, hd), F32),
                        pltpu.VMEM((SUB, hd), F32), pltpu.VMEM((SUB, hd), F32),
                        pltpu.VMEM((1, hd), F32), pltpu.VMEM((1, hd), F32)],
        compiler_params=_params(("arbitrary", "arbitrary")),
        name="mixer_scan",
    )(hq, hkf, hkb, hcf, hcb, hv, hg, hgain, mqk, mqk, conv_w, conv_w, conv_b, conv_b, mv, mo, gates, mgain)


def _attn_kernel(x_ref, hmix_ref, mmix_ref, wout_ref, g_ref, wq_ref, kv_ref, wo_ref, out_ref):
    w = GROUP_WIDTH
    hd = XATTN_HEAD_DIM
    x1 = x_ref[...] + _dot(hmix_ref[...], wout_ref[0:w, :]) + _dot(mmix_ref[...], wout_ref[w:2 * w, :])
    xn = _rms_scale(x1, g_ref[...]).astype(BF16)
    q = (_dot(xn, wq_ref[...]) * (hd ** -0.5)).astype(BF16)
    heads = []
    for h in range(HEADS):
        kh = kv_ref[:, h * hd:(h + 1) * hd]
        vh = kv_ref[:, D_MODEL + h * hd:D_MODEL + (h + 1) * hd]
        sc = _dot_nt(q[:, h * hd:(h + 1) * hd], kh)
        p = jnp.exp(sc - jnp.max(sc, axis=-1, keepdims=True))
        o = _dot(p.astype(BF16), vh) / jnp.sum(p, axis=-1, keepdims=True)
        heads.append(o.astype(BF16))
    out_ref[...] = x1 + _dot(jnp.concatenate(heads, axis=1), wo_ref[...])


def _attn_block(x, hmix, mmix, w_out, norm_xattn, w_xq, kv, w_xo, layer):
    b, s, d = x.shape
    tm = min(WIDE_ROW_TILE, s)
    m = kv.shape[2]
    row = lambda i, j: (i, j, 0)
    fixed3 = lambda i, j: (layer, 0, 0)
    wspec = lambda: pl.BlockSpec((None, d, d), fixed3, pipeline_mode=pl.Buffered(1))
    return pl.pallas_call(
        _attn_kernel,
        grid=(b, s // tm),
        in_specs=[
            pl.BlockSpec((None, tm, d), row),
            pl.BlockSpec((None, tm, GROUP_WIDTH), row),
            pl.BlockSpec((None, tm, GROUP_WIDTH), row),
            wspec(),
            pl.BlockSpec((None, 1, d), fixed3),
            wspec(),
            pl.BlockSpec((None, None, m, 2 * d), lambda i, j: (layer, i, 0, 0)),
            wspec(),
        ],
        out_specs=pl.BlockSpec((None, tm, d), row),
        out_shape=jax.ShapeDtypeStruct((b, s, d), F32),
        compiler_params=_params(("arbitrary", "arbitrary")),
        name="attn_block",
    )(x, hmix, mmix, w_out, norm_xattn, w_xq, kv, w_xo)


def _mlp_kernel(x_ref, g_ref, wup_ref, wdn_ref, gfin_ref, out_ref, *, final):
    x = x_ref[...]
    xn = _rms_scale(x, g_ref[...]).astype(BF16)
    acc = x
    step = D_MODEL
    for c0 in range(0, D_FF, step):
        hcol = jnp.maximum(_dot(xn, wup_ref[:, c0:c0 + step]), 0.0)
        acc = acc + _dot((hcol * hcol).astype(BF16), wdn_ref[c0:c0 + step, :])
    if final:
        acc = _rms_scale(acc, gfin_ref[...])
    out_ref[...] = acc


def _mlp_block(x, norm_mlp, w_up, w_down, norm_final, layer, final):
    b, s, d = x.shape
    tm = min(WIDE_ROW_TILE, s)
    row = lambda i, j: (i, j, 0)
    fixed3 = lambda i, j: (layer, 0, 0)
    return pl.pallas_call(
        functools.partial(_mlp_kernel, final=final),
        grid=(b, s // tm),
        in_specs=[
            pl.BlockSpec((None, tm, d), row),
            pl.BlockSpec((None, 1, d), fixed3),
            pl.BlockSpec((None, d, D_FF), fixed3, pipeline_mode=pl.Buffered(1)),
            pl.BlockSpec((None, D_FF, d), fixed3, pipeline_mode=pl.Buffered(1)),
            pl.BlockSpec((1, d), lambda i, j: (0, 0)),
        ],
        out_specs=pl.BlockSpec((None, tm, d), row),
        out_shape=jax.ShapeDtypeStruct((b, s, d), F32),
        compiler_params=_params(("arbitrary", "arbitrary")),
        name="mlp_block",
    )(x, norm_mlp, w_up, w_down, norm_final)


def kernel(x, mem, norm_mix, norm_xattn, norm_mem, norm_mlp, norm_final, w_in, b_in, hgrn_lb_logits, hgrn_norm,
           mlstm_conv_w, mlstm_conv_b, mlstm_norm, w_out, w_xq, w_xkv, w_xo, w_up, w_down):
    depth = w_in.shape[0]
    _, s, d = x.shape
    assert d == D_MODEL and s % (2 * UNROLL * CHUNK) == 0 and s % min(WIDE_ROW_TILE, s) == 0
    assert w_in.shape[-1] == MAIN_COLS + 4 * HEADS

    row3 = lambda t: t.reshape(depth, 1, t.shape[-1])
    w_gate = w_in[:, :, MAIN_COLS:].reshape(depth, d, 4, HEADS).transpose(0, 3, 2, 1)
    w_gate = jnp.pad(w_gate, ((0, 0), (0, 0), (0, GATE_ROWS - 4), (0, 0))).reshape(depth, HEADS * GATE_ROWS, d)
    b_gate = b_in[:, MAIN_COLS:].reshape(depth, 4, HEADS).transpose(0, 2, 1)
    b_gate = jnp.pad(b_gate, ((0, 0), (0, 0), (0, GATE_ROWS - 4))).reshape(depth, HEADS * GATE_ROWS, 1)
    w_main = w_in.astype(BF16)
    b_main = row3(b_in)
    w_gate = w_gate.astype(BF16)
    w_out_b, w_xq_b, w_xkv_b, w_xo_b = (t.astype(BF16) for t in (w_out, w_xq, w_xkv, w_xo))
    w_up_b, w_down_b = w_up.astype(BF16), w_down.astype(BF16)
    norm_mix3, norm_xattn3, norm_mlp3 = row3(norm_mix), row3(norm_xattn), row3(norm_mlp)
    hgrn_norm3, mlstm_norm3, conv_b3 = row3(hgrn_norm), row3(mlstm_norm), row3(mlstm_conv_b)
    norm_final2 = norm_final.reshape(1, d)

    lb = _lower_bounds(hgrn_lb_logits)
    kv = _mem_kv(mem, norm_mem, w_xkv_b)

    for l in range(depth):
        (hq, hkf, hkb, hcf, hcb, hv, hg, mqk, mv, mo, gates) = _in_proj(
            x, l, norm_mix3, w_main, b_main, w_gate, b_gate, lb)
        hmix, mmix = _mixer_scan(hq, hkf, hkb, hcf, hcb, hv, hg, hgrn_norm3,
                                 mqk, mlstm_conv_w, conv_b3, mv, mo, gates, mlstm_norm3, l)
        x = _attn_block(x, hmix, mmix, w_out_b, norm_xattn3, w_xq_b, kv, w_xo_b, l)
        x = _mlp_block(x, norm_mlp3, w_up_b, w_down_b, norm_final2, l, final=(l == depth - 1))
    return x
```
